```python
import math
import numpy as np
import jax
import jax.numpy as jnp
from jax import lax

D_MODEL = 2048
BATCH = 4
SEQ = 8192
DEPTH = 2

F32 = jnp.float32
HEAD_DIM = 128
EPS = 1e-6
D_FF = 128 * ((8 * D_MODEL // 3 + 127) // 128)

DN_HEADS = 8
DN_WIDTH = DN_HEADS * HEAD_DIM
DN_CONV = 4
DN_CHUNK = 64

NSA_HEADS = 8
NSA_GROUPS = 2
NSA_HPG = NSA_HEADS // NSA_GROUPS
NSA_WIDTH = NSA_HEADS * HEAD_DIM
NSA_KV = NSA_GROUPS * HEAD_DIM
CMP_STRIDE = 16
CMP_BLOCK = 2 * CMP_STRIDE
CMP_HIDDEN = 256
SEL_BLOCK = 64
SEL_TOPN = 16
WINDOW = 512
Q_BLOCK = 128
ALIBI_MAX = 8.0

SGU_GROUPS = 8
SGU_WIDTH = SGU_GROUPS * HEAD_DIM
SGU_CHUNK = 128

IN_SIZES = (DN_WIDTH, DN_WIDTH, DN_WIDTH, DN_WIDTH, DN_HEADS, DN_HEADS,
            NSA_WIDTH, NSA_KV, NSA_KV, NSA_KV, NSA_KV, NSA_KV, NSA_KV, 3 * NSA_HEADS,
            SGU_WIDTH, SGU_WIDTH,
            D_MODEL, D_MODEL, D_MODEL)
D_IN = sum(IN_SIZES)
IN_OFFSETS = tuple(int(o) for o in np.cumsum(IN_SIZES)[:-1])

kernel_name = 'hybrid_deltanet_nsa_gmlp_macaron'


def rms_norm(x, w):
    xf = x.astype(F32)
    y = xf * lax.rsqrt(jnp.mean(xf * xf, axis=-1, keepdims=True) + EPS)
    return (y * w.astype(F32)).astype(x.dtype)


def l2_norm(x):
    xf = x.astype(F32)
    return xf * lax.rsqrt(jnp.sum(xf * xf, axis=-1, keepdims=True) + EPS)


def swiglu(h, w_gate, w_up, w_down):
    return (jax.nn.silu(h @ w_gate) * (h @ w_up)) @ w_down


def causal_depthwise_conv(x, w):
    k, c = w.shape
    return lax.conv_general_dilated(x, w[:, None, :].astype(x.dtype), window_strides=(1,),
                                    padding=[(k - 1, 0)], dimension_numbers=('NWC', 'WIO', 'NWC'),
                                    feature_group_count=c)


def masked_softmax(s, mask):
    p = jax.nn.softmax(jnp.where(mask, s, -1e30), axis=-1)
    return jnp.where(mask, p, 0.0)


def gated_deltanet(q, k, v, z, a, b, conv_w, a_log, dt_bias, out_norm):
    bsz, seq, _ = q.shape
    qkv = jax.nn.silu(causal_depthwise_conv(jnp.concatenate([q, k, v], axis=-1), conv_w))
    q, k, v = jnp.split(qkv.astype(F32), 3, axis=-1)
    heads = lambda t: t.reshape(bsz, seq, DN_HEADS, HEAD_DIM)
    q = l2_norm(heads(q)) * HEAD_DIM ** -0.5
    k = l2_norm(heads(k))
    v = heads(v)
    beta = jax.nn.sigmoid(b.astype(F32))
    g = -jnp.exp(a_log.astype(F32)) * jax.nn.softplus(a.astype(F32) + dt_bias.astype(F32))
    n_chunk = seq // DN_CHUNK

    def to_chunks(t):
        t = t.reshape((bsz, n_chunk, DN_CHUNK) + t.shape[2:])
        return jnp.moveaxis(t, (1, 3), (0, 2))

    qc, kc, vc, bc = to_chunks(q), to_chunks(k), to_chunks(v), to_chunks(beta)
    gc = jnp.cumsum(to_chunks(g), axis=-1)
    idx = jnp.arange(DN_CHUNK)
    incl = idx[:, None] >= idx[None, :]
    strict = idx[:, None] > idx[None, :]
    diff = gc[..., :, None] - gc[..., None, :]
    decay = jnp.where(incl, jnp.exp(jnp.where(incl, diff, 0.0)), 0.0)
    kb = kc * bc[..., None]
    lower = jnp.where(strict, jnp.einsum('nbhid,nbhjd->nbhij', kb, kc) * decay, 0.0)
    eye = jnp.eye(DN_CHUNK, dtype=F32)
    rhs = jnp.concatenate([vc * bc[..., None], kb * jnp.exp(gc)[..., None]], axis=-1)
    sol = lax.linalg.triangular_solve(eye + lower, rhs, left_side=True, lower=True, unit_diagonal=True)
    u, w = jnp.split(sol, [HEAD_DIM], axis=-1)
    attn = jnp.einsum('nbhid,nbhjd->nbhij', qc, kc) * decay
    q_dec = qc * jnp.exp(gc)[..., None]
    k_dec = kc * jnp.exp(gc[..., -1:] - gc)[..., None]
    g_last = jnp.exp(gc[..., -1])

    def step(state, xs):
        u_n, w_n, attn_n, q_n, k_n, gl_n = xs
        v_new = u_n - jnp.einsum('bhck,bhkv->bhcv', w_n, state)
        o = jnp.einsum('bhck,bhkv->bhcv', q_n, state) + jnp.einsum('bhij,bhjv->bhiv', attn_n, v_new)
        state = state * gl_n[..., None, None] + jnp.einsum('bhck,bhcv->bhkv', k_n, v_new)
        return state, o

    s0 = jnp.zeros((bsz, DN_HEADS, HEAD_DIM, HEAD_DIM), F32)
    _, o = lax.scan(step, s0, (u, w, attn, q_dec, k_dec, g_last))
    o = jnp.moveaxis(o, (0, 2), (1, 3)).reshape(bsz, seq, DN_HEADS, HEAD_DIM)
    zg = jax.nn.silu(z.astype(F32).reshape(bsz, seq, DN_HEADS, HEAD_DIM))
    return (rms_norm(o, out_norm) * zg).reshape(bsz, seq, DN_WIDTH)


def native_sparse_attention(q, k_cmp, v_cmp, k_slc, v_slc, k_win, v_win, gate_logits,
                            q_norm, k_norm, cmpk_pos, cmpk_w1, cmpk_w2, cmpv_pos, cmpv_w1, cmpv_w2):
    bsz, seq, _ = q.shape
    G, HPG, DH = NSA_GROUPS, NSA_HPG, HEAD_DIM
    q = (rms_norm(q.reshape(bsz, seq, G, HPG, DH), q_norm).astype(F32) * DH ** -0.5)
    kv = lambda t: t.reshape(bsz, seq, G, DH)

    def compress(t, pos, w1, w2):
        sub = kv(t).reshape(bsz, seq // CMP_STRIDE, CMP_STRIDE, G, DH)
        blk = jnp.concatenate([sub[:, :-1], sub[:, 1:]], axis=2) + pos[None, None, :, None, :]
        blk = jnp.transpose(blk, (0, 1, 3, 2, 4)).reshape(bsz, -1, G, CMP_BLOCK * DH)
        return jax.nn.silu(blk @ w1) @ w2

    kc = rms_norm(compress(k_cmp, cmpk_pos, cmpk_w1, cmpk_w2), k_norm[0]).astype(F32)
    vc = compress(v_cmp, cmpv_pos, cmpv_w1, cmpv_w2).astype(F32)
    n_cmp = kc.shape[1]
    pos_c = jnp.arange(n_cmp) * CMP_STRIDE + (CMP_BLOCK - 1)

    n_sel = seq // SEL_BLOCK
    top_n = min(SEL_TOPN, n_sel)
    sel_blocks = lambda t: t.reshape(bsz, n_sel, SEL_BLOCK, G, DH).transpose(0, 3, 1, 2, 4)
    ks = sel_blocks(rms_norm(kv(k_slc), k_norm[1]).astype(F32))
    vs = sel_blocks(kv(v_slc).astype(F32))
    ci = np.arange(n_cmp)[:, None]
    sj = np.arange(n_sel)[None, :]
    per = SEL_BLOCK // CMP_STRIDE
    overlap = jnp.asarray((ci // per == sj).astype(np.float32) + ((ci + 1) // per == sj).astype(np.float32))

    pad = ((0, 0), (WINDOW, 0), (0, 0), (0, 0))
    kw = jnp.pad(rms_norm(kv(k_win), k_norm[2]).astype(F32), pad)
    vw = jnp.pad(kv(v_win).astype(F32), pad)
    gates = jax.nn.sigmoid(gate_logits.astype(F32)).reshape(bsz, seq, 3, G, HPG)
    slopes = (2.0 ** (-ALIBI_MAX * jnp.arange(1, NSA_HEADS + 1, dtype=F32) / NSA_HEADS)).reshape(G, HPG)
    sl = slopes[None, :, :, None, None]
    bi = jnp.arange(bsz)[:, None, None]
    gi = jnp.arange(G)[None, :, None]
    jsel = jnp.arange(n_sel)

    def block(qb):
        t0 = qb * Q_BLOCK
        t = t0 + jnp.arange(Q_BLOCK)
        qblk = lax.dynamic_slice_in_dim(q, t0, Q_BLOCK, axis=1)
        gblk = lax.dynamic_slice_in_dim(gates, t0, Q_BLOCK, axis=1)
        dist_c = t[:, None] - pos_c[None, :]
        s_c = jnp.einsum('bqghd,bcgd->bghqc', qblk, kc) - sl * dist_c.astype(F32)
        p_c = masked_softmax(s_c, dist_c >= 0)
        o_c = jnp.einsum('bghqc,bcgd->bqghd', p_c, vc)
        imp = jnp.einsum('bghqc,cj->bqgj', p_c, overlap)
        cur = t // SEL_BLOCK
        causal_j = jsel[None, :] <= cur[:, None]
        forced = (jsel[None, :] == 0) | (jsel[None, :] == cur[:, None]) | (jsel[None, :] == cur[:, None] - 1)
        score = jnp.where(forced[:, None, :], 1e6, jnp.where(causal_j[:, None, :], imp, -1e6))
        _, idx = lax.top_k(score, top_n)
        idx = jnp.transpose(idx, (0, 2, 1, 3)).reshape(bsz, G, Q_BLOCK * top_n)
        k_sel = ks[bi, gi, idx].reshape(bsz, G, Q_BLOCK, top_n * SEL_BLOCK, DH)
        v_sel = vs[bi, gi, idx].reshape(bsz, G, Q_BLOCK, top_n * SEL_BLOCK, DH)
        kpos = (idx.reshape(bsz, G, Q_BLOCK, top_n, 1) * SEL_BLOCK + jnp.arange(SEL_BLOCK)).reshape(bsz, G, Q_BLOCK, top_n * SEL_BLOCK)
        dist_s = (t[None, None, :, None] - kpos)[:, :, None]
        s_s = jnp.einsum('bqghd,bgqkd->bghqk', qblk, k_sel) - sl * dist_s.astype(F32)
        p_s = masked_softmax(s_s, dist_s >= 0)
        o_s = jnp.einsum('bghqk,bgqkd->bqghd', p_s, v_sel)
        kwb = lax.dynamic_slice_in_dim(kw, t0, WINDOW + Q_BLOCK, axis=1)
        vwb = lax.dynamic_slice_in_dim(vw, t0, WINDOW + Q_BLOCK, axis=1)
        s_pos = t0 - WINDOW + jnp.arange(WINDOW + Q_BLOCK)
        dist_w = t[:, None] - s_pos[None, :]
        mask_w = (dist_w >= 0) & (dist_w < WINDOW) & (s_pos[None, :] >= 0)
        s_w = jnp.einsum('bqghd,bkgd->bghqk', qblk, kwb) - sl * dist_w.astype(F32)
        p_w = masked_softmax(s_w, mask_w)
        o_w = jnp.einsum('bghqk,bkgd->bqghd', p_w, vwb)
        o = gblk[:, :, 0, :, :, None] * o_c + gblk[:, :, 1, :, :, None] * o_s + gblk[:, :, 2, :, :, None] * o_w
        return o.reshape(bsz, Q_BLOCK, NSA_WIDTH)

    out = lax.map(block, jnp.arange(seq // Q_BLOCK))
    return jnp.transpose(out, (1, 0, 2, 3)).reshape(bsz, seq, NSA_WIDTH)


def spatial_gating_unit(u, v, norm_w, norm_b, w_s, b_s):
    bsz, seq, _ = u.shape
    u = jax.nn.gelu(u.astype(F32))
    vf = jax.nn.gelu(v.astype(F32))
    mu = jnp.mean(vf, axis=-1, keepdims=True)
    var = jnp.mean(jnp.square(vf - mu), axis=-1, keepdims=True)
    vn = (vf - mu) * lax.rsqrt(var + EPS) * norm_w.astype(F32) + norm_b.astype(F32)
    vn = vn.reshape(bsz, seq // SGU_CHUNK, SGU_CHUNK, SGU_GROUPS, HEAD_DIM)
    causal = np.tril(np.ones((SGU_CHUNK, SGU_CHUNK), dtype=bool))
    w = jnp.where(causal, w_s.astype(F32), 0.0)
    s = jnp.einsum('gij,bcjgd->bcigd', w, vn) + b_s.astype(F32).T[None, None, :, :, None]
    return u * s.reshape(bsz, seq, SGU_WIDTH)


def hybrid_layer(x, ffn1_norm, ffn1_gate, ffn1_up, ffn1_down, mix_norm, w_in,
                 dn_conv, dn_a_log, dn_dt_bias, dn_out_norm,
                 nsa_q_norm, nsa_k_norm, cmpk_pos, cmpk_w1, cmpk_w2, cmpv_pos, cmpv_w1, cmpv_w2,
                 sgu_norm_w, sgu_norm_b, sgu_w, sgu_b,
                 w_branch_a, w_branch_b, w_branch_c, w_out,
                 ffn2_norm, ffn2_gate, ffn2_up, ffn2_down):
    x = x + 0.5 * swiglu(rms_norm(x, ffn1_norm), ffn1_gate, ffn1_up, ffn1_down)
    h = rms_norm(x, mix_norm)
    (dq, dk, dv, dz, da, db, nq, nkc, nvc, nks, nvs, nkw, nvw, ngate,
     su, sv, ga, gb, gc) = jnp.split(h @ w_in, IN_OFFSETS, axis=-1)
    y_a = gated_deltanet(dq, dk, dv, dz, da, db, dn_conv, dn_a_log, dn_dt_bias, dn_out_norm)
    y_b = native_sparse_attention(nq, nkc, nvc, nks, nvs, nkw, nvw, ngate, nsa_q_norm, nsa_k_norm,
                                  cmpk_pos, cmpk_w1, cmpk_w2, cmpv_pos, cmpv_w1, cmpv_w2)
    y_c = spatial_gating_unit(su, sv, sgu_norm_w, sgu_norm_b, sgu_w, sgu_b)
    merged = (jax.nn.sigmoid(ga) * (y_a.astype(x.dtype) @ w_branch_a)
              + jax.nn.sigmoid(gb) * (y_b.astype(x.dtype) @ w_branch_b)
              + jax.nn.sigmoid(gc) * (y_c.astype(x.dtype) @ w_branch_c))
    x = x + merged @ w_out
    x = x + 0.5 * swiglu(rms_norm(x, ffn2_norm), ffn2_gate, ffn2_up, ffn2_down)
    return x


def setup_inputs(seed: int = 0) -> dict:
    key = jax.random.key(seed)
    ks = jax.random.split(key, 32)
    L = DEPTH

    def nrm(k, shape, scale):
        return jax.random.normal(k, shape, F32) * scale

    def gain(k, shape):
        return 1.0 + 0.02 * jax.random.normal(k, shape, F32)

    dt = jnp.exp(jax.random.uniform(ks[9], (L, DN_HEADS), F32, math.log(1e-3), math.log(1e-1)))
    return {
        'x': nrm(ks[0], (BATCH, SEQ, D_MODEL), 1.0),
        'ffn1_norm': gain(ks[1], (L, D_MODEL)),
        'ffn1_gate': nrm(ks[2], (L, D_MODEL, D_FF), D_MODEL ** -0.5),
        'ffn1_up': nrm(ks[3], (L, D_MODEL, D_FF), D_MODEL ** -0.5),
        'ffn1_down': nrm(ks[4], (L, D_FF, D_MODEL), D_FF ** -0.5),
        'mix_norm': gain(ks[5], (L, D_MODEL)),
        'w_in': nrm(ks[6], (L, D_MODEL, D_IN), D_MODEL ** -0.5),
        'dn_conv': nrm(ks[7], (L, DN_CONV, 3 * DN_WIDTH), DN_CONV ** -0.5),
        'dn_a_log': jnp.log(jax.random.uniform(ks[8], (L, DN_HEADS), F32, 1.0, 16.0)),
        'dn_dt_bias': dt + jnp.log(-jnp.expm1(-dt)),
        'dn_out_norm': gain(ks[10], (L, HEAD_DIM)),
        'nsa_q_norm': gain(ks[11], (L, HEAD_DIM)),
        'nsa_k_norm': gain(ks[12], (L, 3, HEAD_DIM)),
        'cmpk_pos': nrm(ks[13], (L, CMP_BLOCK, HEAD_DIM), 0.02),
        'cmpk_w1': nrm(ks[14], (L, CMP_BLOCK * HEAD_DIM, CMP_HIDDEN), (CMP_BLOCK * HEAD_DIM) ** -0.5),
        'cmpk_w2': nrm(ks[15], (L, CMP_HIDDEN, HEAD_DIM), CMP_HIDDEN ** -0.5),
        'cmpv_pos': nrm(ks[16], (L, CMP_BLOCK, HEAD_DIM), 0.02),
        'cmpv_w1': nrm(ks[17], (L, CMP_BLOCK * HEAD_DIM, CMP_HIDDEN), (CMP_BLOCK * HEAD_DIM) ** -0.5),
        'cmpv_w2': nrm(ks[18], (L, CMP_HIDDEN, HEAD_DIM), CMP_HIDDEN ** -0.5),
        'sgu_norm_w': gain(ks[19], (L, SGU_WIDTH)),
        'sgu_norm_b': nrm(ks[20], (L, SGU_WIDTH), 0.02),
        'sgu_w': nrm(ks[21], (L, SGU_GROUPS, SGU_CHUNK, SGU_CHUNK), SGU_CHUNK ** -0.5),
        'sgu_b': gain(ks[22], (L, SGU_GROUPS, SGU_CHUNK)),
        'w_branch_a': nrm(ks[23], (L, DN_WIDTH, D_MODEL), DN_WIDTH ** -0.5),
        'w_branch_b': nrm(ks[24], (L, NSA_WIDTH, D_MODEL), NSA_WIDTH ** -0.5),
        'w_branch_c': nrm(ks[25], (L, SGU_WIDTH, D_MODEL), SGU_WIDTH ** -0.5),
        'w_out': nrm(ks[26], (L, D_MODEL, D_MODEL), D_MODEL ** -0.5),
        'ffn2_norm': gain(ks[27], (L, D_MODEL)),
        'ffn2_gate': nrm(ks[28], (L, D_MODEL, D_FF), D_MODEL ** -0.5),
        'ffn2_up': nrm(ks[29], (L, D_MODEL, D_FF), D_MODEL ** -0.5),
        'ffn2_down': nrm(ks[30], (L, D_FF, D_MODEL), D_FF ** -0.5),
    }


def reference(x, ffn1_norm, ffn1_gate, ffn1_up, ffn1_down, mix_norm, w_in,
              dn_conv, dn_a_log, dn_dt_bias, dn_out_norm,
              nsa_q_norm, nsa_k_norm, cmpk_pos, cmpk_w1, cmpk_w2, cmpv_pos, cmpv_w1, cmpv_w2,
              sgu_norm_w, sgu_norm_b, sgu_w, sgu_b,
              w_branch_a, w_branch_b, w_branch_c, w_out,
              ffn2_norm, ffn2_gate, ffn2_up, ffn2_down):
    for l in range(DEPTH):
        x = hybrid_layer(x, ffn1_norm[l], ffn1_gate[l], ffn1_up[l], ffn1_down[l], mix_norm[l], w_in[l],
                         dn_conv[l], dn_a_log[l], dn_dt_bias[l], dn_out_norm[l],
                         nsa_q_norm[l], nsa_k_norm[l], cmpk_pos[l], cmpk_w1[l], cmpk_w2[l],
                         cmpv_pos[l], cmpv_w1[l], cmpv_w2[l],
                         sgu_norm_w[l], sgu_norm_b[l], sgu_w[l], sgu_b[l],
                         w_branch_a[l], w_branch_b[l], w_branch_c[l], w_out[l],
                         ffn2_norm[l], ffn2_gate[l], ffn2_up[l], ffn2_down[l])
    return x
```

```python
import functools

import jax
import jax.numpy as jnp
import numpy as np
from jax import lax
from jax.experimental import pallas as pl
from jax.experimental.pallas import tpu as pltpu

F32 = jnp.float32
BF16 = jnp.bfloat16
HIGHEST = lax.Precision.HIGHEST

LANES = 128
V7X_VMEM_LIMIT = 56 * 1024 * 1024

HEAD_DIM = 128
EPS = 1e-6
NEG = -1e30

DN_HEADS = 8
DN_WIDTH = DN_HEADS * HEAD_DIM
DN_CONV = 4
DN_CHUNK = 64
DN_TILE = 2 * DN_CHUNK
DN_HALO = 8

NSA_HEADS = 8
NSA_GROUPS = 2
NSA_HPG = NSA_HEADS // NSA_GROUPS
NSA_WIDTH = NSA_HEADS * HEAD_DIM
CMP_STRIDE = 16
CMP_BLOCK = 2 * CMP_STRIDE
CMP_HIDDEN = 256
SEL_BLOCK = 64
SEL_TOPN = 16
WINDOW = 512
Q_BLOCK = 128
ALIBI_MAX = 8.0
SEL_KT = 512

SGU_GROUPS = 8
SGU_WIDTH = SGU_GROUPS * HEAD_DIM
SGU_CHUNK = 128


def _cparams(sem):
    return pltpu.CompilerParams(dimension_semantics=sem, vmem_limit_bytes=V7X_VMEM_LIMIT)


def _rms(x, w):
    return x * lax.rsqrt(jnp.mean(x * x, axis=-1, keepdims=True) + EPS) * w


def _iota(shape, dim):
    return lax.broadcasted_iota(jnp.int32, shape, dim)


def _dot(a, b, precision=None):
    return jnp.dot(a, b, preferred_element_type=F32, precision=precision)


def _dot_nt(a, b, precision=None):
    return lax.dot_general(a, b, (((1,), (1,)), ((), ())), preferred_element_type=F32,
                           precision=precision)


def _ffn_body(x_ref, nw_ref, wg_ref, wu_ref, wd_ref, o_ref, h_ref):
    @pl.when(pl.program_id(1) == 0)
    def _():
        x = x_ref[...]
        h_ref[...] = _rms(x, nw_ref[...]).astype(BF16)
        o_ref[...] = x

    h = h_ref[...]
    g = _dot(h, wg_ref[...])
    u = _dot(h, wu_ref[...])
    a = (0.5 * g * jax.nn.sigmoid(g) * u).astype(BF16)
    o_ref[...] += _dot(a, wd_ref[...])


def _ffn(x, norm_w, wg, wu, wd, *, tm=512, tf=512):
    t, d = x.shape
    ff = wg.shape[1]
    ffp = tf * pl.cdiv(ff, tf)
    wg = jnp.pad(wg.astype(BF16), ((0, 0), (0, ffp - ff)))
    wu = jnp.pad(wu.astype(BF16), ((0, 0), (0, ffp - ff)))
    wd = jnp.pad(wd.astype(BF16), ((0, ffp - ff), (0, 0)))
    return pl.pallas_call(
        _ffn_body,
        out_shape=jax.ShapeDtypeStruct((t, d), F32),
        grid=(t // tm, ffp // tf),
        in_specs=[
            pl.BlockSpec((tm, d), lambda i, j: (i, 0)),
            pl.BlockSpec((1, d), lambda i, j: (0, 0)),
            pl.BlockSpec((d, tf), lambda i, j: (0, j)),
            pl.BlockSpec((d, tf), lambda i, j: (0, j)),
            pl.BlockSpec((tf, d), lambda i, j: (j, 0)),
        ],
        out_specs=pl.BlockSpec((tm, d), lambda i, j: (i, 0)),
        scratch_shapes=[pltpu.VMEM((tm, d), BF16)],
        compiler_params=_cparams(("parallel", "arbitrary")),
        name="ffn",
    )(x, norm_w.reshape(1, d), wg, wu, wd)


C_DQ, C_DK, C_DV, C_DZ = 0, 1024, 2048, 3072
C_NQ = 4096
C_SU, C_SV = 5120, 6144
C_NKV = 7168
C_GA, C_GB, C_GC = 8704, 10752, 12800
C_SMALL = 14848
P_COLS = 15360


def _reorder_w_in(w_in):
    d = w_in.shape[0]
    pad = jnp.zeros((d, P_COLS - 14888), w_in.dtype)
    return jnp.concatenate([
        w_in[:, 0:4096],
        w_in[:, 4112:5136],
        w_in[:, 6696:8744],
        w_in[:, 5136:6672],
        w_in[:, 8744:14888],
        w_in[:, 4096:4112],
        w_in[:, 6672:6696],
        pad], axis=1).astype(BF16)


def _inproj_body(x_ref, nw_ref, w_ref, o_ref, h_ref):
    @pl.when(pl.program_id(1) == 0)
    def _():
        h_ref[...] = _rms(x_ref[...], nw_ref[...]).astype(BF16)

    o_ref[...] = _dot(h_ref[...], w_ref[...]).astype(o_ref.dtype)


def _inproj(x, norm_w, w, *, tm=1024, tn=1024):
    t, d = x.shape
    n = w.shape[1]
    return pl.pallas_call(
        _inproj_body,
        out_shape=jax.ShapeDtypeStruct((t, n), BF16),
        grid=(t // tm, n // tn),
        in_specs=[
            pl.BlockSpec((tm, d), lambda i, j: (i, 0)),
            pl.BlockSpec((1, d), lambda i, j: (0, 0)),
            pl.BlockSpec((d, tn), lambda i, j: (0, j)),
        ],
        out_specs=pl.BlockSpec((tm, tn), lambda i, j: (i, j)),
        scratch_shapes=[pltpu.VMEM((tm, d), BF16)],
        compiler_params=_cparams(("parallel", "arbitrary")),
        name="inproj",
    )(x, norm_w.reshape(1, d), w)


def _merge_body(x_ref, ya_ref, yb_ref, yc_ref, ga_ref, gb_ref, gc_ref,
                wa_ref, wb_ref, wc_ref, wo_ref, o_ref):
    @pl.when(pl.program_id(1) == 0)
    def _():
        o_ref[...] = x_ref[...]

    m = jax.nn.sigmoid(ga_ref[...].astype(F32)) * _dot(ya_ref[...], wa_ref[...])
    m += jax.nn.sigmoid(gb_ref[...].astype(F32)) * _dot(yb_ref[...], wb_ref[...])
    m += jax.nn.sigmoid(gc_ref[...].astype(F32)) * _dot(yc_ref[...], wc_ref[...])
    o_ref[...] += _dot(m.astype(BF16), wo_ref[...])


def _merge(x, p, ya, yb, yc, wa, wb, wc, wo, *, tm=512, tn=512):
    t, d = x.shape
    k = ya.shape[1]
    y_spec = pl.BlockSpec((tm, k), lambda i, j: (i, 0))
    w_spec = pl.BlockSpec((k, tn), lambda i, j: (0, j))

    def gate_spec(col):
        return pl.BlockSpec((tm, tn), lambda i, j: (i, col // tn + j))

    return pl.pallas_call(
        _merge_body,
        out_shape=jax.ShapeDtypeStruct((t, d), F32),
        grid=(t // tm, d // tn),
        in_specs=[pl.BlockSpec((tm, d), lambda i, j: (i, 0)), y_spec, y_spec, y_spec,
                  gate_spec(C_GA), gate_spec(C_GB), gate_spec(C_GC),
                  w_spec, w_spec, w_spec,
                  pl.BlockSpec((tn, d), lambda i, j: (j, 0))],
        out_specs=pl.BlockSpec((tm, d), lambda i, j: (i, 0)),
        compiler_params=_cparams(("parallel", "arbitrary")),
        name="merge",
    )(x, ya, yb, yc, p, p, p, wa.astype(BF16), wb.astype(BF16), wc.astype(BF16), wo.astype(BF16))


def _sgu_body(u_ref, v_ref, nw_ref, nb_ref, ws_ref, bs_ref, o_ref, *, n_chunk):
    v = jax.nn.gelu(v_ref[...].astype(F32))
    mu = jnp.mean(v, axis=-1, keepdims=True)
    vc = v - mu
    var = jnp.mean(vc * vc, axis=-1, keepdims=True)
    vn = (vc * lax.rsqrt(var + EPS) * nw_ref[...] + nb_ref[...]).astype(BF16)
    tril = _iota((SGU_CHUNK, SGU_CHUNK), 0) >= _iota((SGU_CHUNK, SGU_CHUNK), 1)
    for g in range(SGU_GROUPS):
        cols = slice(g * HEAD_DIM, (g + 1) * HEAD_DIM)
        w = jnp.where(tril, ws_ref[g], 0.0).astype(BF16)
        rhs = jnp.concatenate(
            [vn[c * SGU_CHUNK:(c + 1) * SGU_CHUNK, cols] for c in range(n_chunk)], axis=1)
        s = _dot(w, rhs)
        for c in range(n_chunk):
            rows = slice(c * SGU_CHUNK, (c + 1) * SGU_CHUNK)
            u = jax.nn.gelu(u_ref[rows, cols].astype(F32))
            sc = s[:, c * HEAD_DIM:(c + 1) * HEAD_DIM] + bs_ref[g]
            o_ref[rows, cols] = (u * sc).astype(o_ref.dtype)


def _sgu(p, norm_w, norm_b, w_s, b_s, *, n_chunk=4):
    t = p.shape[0]
    ts = n_chunk * SGU_CHUNK
    bias = jnp.broadcast_to(b_s[:, :, None], (SGU_GROUPS, SGU_CHUNK, HEAD_DIM))
    return pl.pallas_call(
        functools.partial(_sgu_body, n_chunk=n_chunk),
        out_shape=jax.ShapeDtypeStruct((t, SGU_WIDTH), BF16),
        grid=(t // ts,),
        in_specs=[
            pl.BlockSpec((ts, SGU_WIDTH), lambda i: (i, C_SU // SGU_WIDTH)),
            pl.BlockSpec((ts, SGU_WIDTH), lambda i: (i, C_SV // SGU_WIDTH)),
            pl.BlockSpec((1, SGU_WIDTH), lambda i: (0, 0)),
            pl.BlockSpec((1, SGU_WIDTH), lambda i: (0, 0)),
            pl.BlockSpec((SGU_GROUPS, SGU_CHUNK, SGU_CHUNK), lambda i: (0, 0, 0)),
            pl.BlockSpec((SGU_GROUPS, SGU_CHUNK, HEAD_DIM), lambda i: (0, 0, 0)),
        ],
        out_specs=pl.BlockSpec((ts, SGU_WIDTH), lambda i: (i, 0)),
        compiler_params=_cparams(("parallel",)),
        name="sgu",
    )(p, p, norm_w.reshape(1, -1), norm_b.reshape(1, -1), w_s, bias)


def _softplus(x):
    return jnp.maximum(x, 0.0) + jnp.log(1.0 + jnp.exp(-jnp.abs(x)))


def _unit_lower_solve(low, rhs, precision):
    m = -low
    x = rhs + _dot(m, rhs, precision)
    for _ in range(5):
        m = _dot(m, m, precision)
        x = x + _dot(m, x, precision)
    return x


def _dn_body(q_ref, k_ref, v_ref, z_ref, sm_ref, conv_ref, alog_ref, dtb_ref, onorm_ref,
             o_ref, xbuf, s_ref, *, solve_precision):
    n = DN_TILE
    shape = (n, n)

    @pl.when(pl.program_id(1) == 0)
    def _():
        xbuf[0:DN_HALO, :] = jnp.zeros((DN_HALO, 3 * DN_WIDTH), F32)
        s_ref[...] = jnp.zeros_like(s_ref)

    xbuf[DN_HALO:, 0:DN_WIDTH] = q_ref[...].astype(F32)
    xbuf[DN_HALO:, DN_WIDTH:2 * DN_WIDTH] = k_ref[...].astype(F32)
    xbuf[DN_HALO:, 2 * DN_WIDTH:] = v_ref[...].astype(F32)

    def conv_silu(c0):
        cols = slice(c0, c0 + HEAD_DIM)
        acc = conv_ref[DN_CONV - 1:DN_CONV, cols] * xbuf[DN_HALO:DN_HALO + n, cols]
        for s in range(1, DN_CONV):
            acc += (conv_ref[DN_CONV - 1 - s:DN_CONV - s, cols]
                    * xbuf[DN_HALO - s:DN_HALO - s + n, cols])
        return acc * jax.nn.sigmoid(acc)

    row = _iota(shape, 0)
    col = _iota(shape, 1)
    same = lax.shift_right_logical(row, 6) == lax.shift_right_logical(col, 6)
    incl = same & (row >= col)
    strict = same & (row > col)
    first_chunk_cols = col < DN_CHUNK
    eye = jnp.where(row == col, 1.0, 0.0).astype(BF16)

    sm = sm_ref[...].astype(F32)
    g_all = -jnp.exp(alog_ref[...]) * _softplus(sm + dtb_ref[...])
    beta_all = jax.nn.sigmoid(sm)
    tri = jnp.where(incl, 1.0, 0.0)
    gc_all = _dot(tri, g_all, HIGHEST)
    last = jnp.where(col == jnp.where(row >= DN_CHUNK, n - 1, DN_CHUNK - 1), 1.0, 0.0)
    gl_all = _dot(last, gc_all, HIGHEST)
    gc_t = gc_all.T

    for h in range(DN_HEADS):
        hc = slice(h * HEAD_DIM, (h + 1) * HEAD_DIM)
        qa = conv_silu(h * HEAD_DIM)
        ka = conv_silu(DN_WIDTH + h * HEAD_DIM)
        va = conv_silu(2 * DN_WIDTH + h * HEAD_DIM)
        qn = qa * lax.rsqrt(jnp.sum(qa * qa, axis=-1, keepdims=True) + EPS) * HEAD_DIM ** -0.5
        kn = ka * lax.rsqrt(jnp.sum(ka * ka, axis=-1, keepdims=True) + EPS)
        beta = jnp.broadcast_to(beta_all[:, DN_HEADS + h:DN_HEADS + h + 1], shape)
        g_col = jnp.broadcast_to(gc_all[:, h:h + 1], shape)
        g_row = jnp.broadcast_to(gc_t[h:h + 1, :], shape)
        g_end = jnp.broadcast_to(gl_all[:, h:h + 1], shape)
        decay = jnp.where(incl, jnp.exp(jnp.where(incl, g_col - g_row, 0.0)), 0.0)
        e_col = jnp.exp(g_col)

        knb = kn.astype(BF16)
        kk = _dot_nt(knb, knb)
        qk = _dot_nt(qn.astype(BF16), knb)
        low = jnp.where(strict, beta * kk * decay, 0.0)
        attn = qk * decay
        kb = kn * beta
        rhs = jnp.concatenate([va * beta, kb * e_col], axis=1)
        sol = _unit_lower_solve(low, rhs, solve_precision)
        u = sol[:, :HEAD_DIM]
        w = sol[:, HEAD_DIM:].astype(BF16)
        q_dec = (qn * e_col).astype(BF16)
        kd_t = _dot_nt(eye, (kn * jnp.exp(g_end - g_col)).astype(BF16))
        kd_t0 = jnp.where(first_chunk_cols, kd_t, 0.0).astype(BF16)
        kd_t1 = jnp.where(first_chunk_cols, 0.0, kd_t).astype(BF16)
        decay_end = jnp.exp(g_end)
        c = DN_CHUNK

        s0 = s_ref[h]
        s0b = s0.astype(BF16)
        v0 = u[:c] - _dot(w[:c], s0b)
        o0 = _dot(q_dec[:c], s0b)
        vfull0 = jnp.concatenate([v0, jnp.zeros_like(v0)], axis=0).astype(BF16)
        s1 = s0 * decay_end[0:1, :] + _dot(kd_t0, vfull0)
        s1b = s1.astype(BF16)
        v1 = u[c:] - _dot(w[c:], s1b)
        o1 = _dot(q_dec[c:], s1b)
        vfull = jnp.concatenate([v0, v1], axis=0).astype(BF16)
        s_ref[h] = s1 * decay_end[c:c + 1, :] + _dot(kd_t1, vfull)
        o = jnp.concatenate([o0, o1], axis=0) + _dot(attn.astype(BF16), vfull)

        z = z_ref[:, hc].astype(F32)
        o_ref[:, hc] = (_rms(o, onorm_ref[...]) * (z * jax.nn.sigmoid(z))).astype(o_ref.dtype)

    xbuf[0:DN_HALO, :] = xbuf[n:n + DN_HALO, :]


def _deltanet(p3, conv_w, a_log, dt_bias, out_norm, *, solve_precision=HIGHEST):
    b, s, _ = p3.shape
    wide = lambda c: pl.BlockSpec((None, DN_TILE, DN_WIDTH), lambda i, t: (i, t, c // DN_WIDTH))
    lane_pad = lambda a: jnp.pad(a.reshape(1, -1), ((0, 0), (0, LANES - a.shape[-1])))
    full = lambda shp: pl.BlockSpec(shp, lambda i, t: (0,) * len(shp))
    return pl.pallas_call(
        functools.partial(_dn_body, solve_precision=solve_precision),
        out_shape=jax.ShapeDtypeStruct((b, s, DN_WIDTH), BF16),
        grid=(b, s // DN_TILE),
        in_specs=[wide(C_DQ), wide(C_DK), wide(C_DV), wide(C_DZ),
                  pl.BlockSpec((None, DN_TILE, LANES), lambda i, t: (i, t, C_SMALL // LANES)),
                  full((DN_CONV, 3 * DN_WIDTH)), full((1, LANES)), full((1, LANES)),
                  full((1, HEAD_DIM))],
        out_specs=pl.BlockSpec((None, DN_TILE, DN_WIDTH), lambda i, t: (i, t, 0)),
        scratch_shapes=[pltpu.VMEM((DN_HALO + DN_TILE, 3 * DN_WIDTH), F32),
                        pltpu.VMEM((DN_HEADS, HEAD_DIM, HEAD_DIM), F32)],
        compiler_params=_cparams(("parallel", "arbitrary")),
        name="deltanet",
    )(p3, p3, p3, p3, p3, conv_w, lane_pad(a_log), lane_pad(dt_bias), out_norm.reshape(1, -1))


def _nsa_prep_body(yk_ref, yv_ref, ks_ref, vs_ref, kw_ref, vw_ref,
                   kpos_ref, kw1_ref, kw2_ref, vpos_ref, vw1_ref, vw2_ref, knorm_ref,
                   kc_o, vc_o, ks_o, vs_o, kw_o, vw_o):
    def compress(y_ref, pos_ref, w1_ref, w2_ref):
        y = y_ref[...].astype(F32)
        a = _dot((y + pos_ref[0:1, :]).astype(BF16), w1_ref[0])
        b = _dot((y + pos_ref[1:2, :]).astype(BF16), w1_ref[1])
        hid = a + pltpu.roll(b, b.shape[0] - 1, axis=0)
        hid = hid * jax.nn.sigmoid(hid)
        return _dot(hid.astype(BF16), w2_ref[...])

    kc_o[...] = _rms(compress(yk_ref, kpos_ref, kw1_ref, kw2_ref), knorm_ref[0:1, :]).astype(BF16)
    vc_o[...] = compress(yv_ref, vpos_ref, vw1_ref, vw2_ref).astype(BF16)
    ks_o[...] = _rms(ks_ref[...].astype(F32), knorm_ref[1:2, :]).astype(BF16)
    vs_o[...] = vs_ref[...]
    kw_o[...] = _rms(kw_ref[...].astype(F32), knorm_ref[2:3, :]).astype(BF16)
    vw_o[...] = vw_ref[...]


def _nsa_prep(p3, k_norm, cmpk_pos, cmpk_w1, cmpk_w2, cmpv_pos, cmpv_w1, cmpv_w2):
    b, s, _ = p3.shape
    g = NSA_GROUPS
    n_sub = s // CMP_STRIDE
    half = CMP_STRIDE * HEAD_DIM

    def stride_chunks(c0):
        y = p3[:, :, c0:c0 + g * HEAD_DIM].reshape(b, n_sub, CMP_STRIDE, g, HEAD_DIM)
        return jnp.transpose(y, (0, 3, 1, 2, 4)).reshape(b, g, n_sub, half)

    col = lambda c0: pl.BlockSpec((None, s, HEAD_DIM), lambda i, j: (i, 0, c0 // HEAD_DIM + j))
    full = lambda shp: pl.BlockSpec(shp, lambda i, j: (0,) * len(shp))
    y_spec = pl.BlockSpec((None, None, n_sub, half), lambda i, j: (i, j, 0, 0))
    c_out = pl.BlockSpec((None, None, n_sub, HEAD_DIM), lambda i, j: (i, j, 0, 0))
    s_out = pl.BlockSpec((None, None, s, HEAD_DIM), lambda i, j: (i, j, 0, 0))
    c_shape = jax.ShapeDtypeStruct((b, g, n_sub, HEAD_DIM), BF16)
    s_shape = jax.ShapeDtypeStruct((b, g, s, HEAD_DIM), BF16)
    kv = g * HEAD_DIM
    return pl.pallas_call(
        _nsa_prep_body,
        out_shape=(c_shape, c_shape, s_shape, s_shape, s_shape, s_shape),
        grid=(b, g),
        in_specs=[y_spec, y_spec,
                  col(C_NKV + 2 * kv), col(C_NKV + 3 * kv), col(C_NKV + 4 * kv), col(C_NKV + 5 * kv),
                  full((2, half)), full((2, half, CMP_HIDDEN)), full((CMP_HIDDEN, HEAD_DIM)),
                  full((2, half)), full((2, half, CMP_HIDDEN)), full((CMP_HIDDEN, HEAD_DIM)),
                  full((3, HEAD_DIM))],
        out_specs=(c_out, c_out, s_out, s_out, s_out, s_out),
        compiler_params=_cparams(("parallel", "parallel")),
        name="nsa_prep",
    )(stride_chunks(C_NKV), stride_chunks(C_NKV + kv), p3, p3, p3, p3,
      cmpk_pos.reshape(2, half), cmpk_w1.reshape(2, half, CMP_HIDDEN).astype(BF16),
      cmpk_w2.astype(BF16),
      cmpv_pos.reshape(2, half), cmpv_w1.reshape(2, half, CMP_HIDDEN).astype(BF16),
      cmpv_w2.astype(BF16), k_norm)


def _nsa_body(q_ref, sm_ref, kc_ref, vc_ref, ks_ref, vs_ref, kw_ref, vw_ref,
              qn_ref, ov_ref, e_ref, o_ref, m_ref, l_ref, acc_ref, *, top_n):
    grp = pl.program_id(1)
    qb = pl.program_id(2)
    t0 = qb * Q_BLOCK
    nq = Q_BLOCK
    nr = NSA_HPG * nq
    ncp = kc_ref.shape[0]
    hrows = [slice(h * nq, (h + 1) * nq) for h in range(NSA_HPG)]

    q = jnp.concatenate(
        [(_rms(q_ref[:, h * HEAD_DIM:(h + 1) * HEAD_DIM].astype(F32), qn_ref[...])
          * HEAD_DIM ** -0.5).astype(BF16) for h in range(NSA_HPG)], axis=0)

    head = (lax.shift_right_logical(_iota((nr, 1), 0), 7) + grp * NSA_HPG + 1).astype(F32)
    slope = jnp.exp(head * (-ALIBI_MAX / NSA_HEADS * np.log(2.0)))

    def softmax_rows(s, mask):
        s = jnp.where(mask, s, NEG)
        e = jnp.exp(s - jnp.max(s, axis=-1, keepdims=True))
        return jnp.where(mask, e / jnp.sum(e, axis=-1, keepdims=True), 0.0)

    tq_c = t0 + (_iota((nr, ncp), 0) & (nq - 1))
    dist_c = tq_c - (_iota((nr, ncp), 1) * CMP_STRIDE + (CMP_BLOCK - 1))
    s_c = _dot_nt(q, kc_ref[...]) - slope * dist_c.astype(F32)
    p_c = softmax_rows(s_c, dist_c >= 0)
    o_c = _dot(p_c.astype(BF16), vc_ref[...])

    p_grp = p_c[hrows[0]]
    for h in range(1, NSA_HPG):
        p_grp = p_grp + p_c[hrows[h]]
    imp = _dot(p_grp, ov_ref[...], HIGHEST)
    jj = _iota((nq, LANES), 1)
    cur = lax.shift_right_logical(t0 + _iota((nq, LANES), 0), 6)
    forced = (jj == 0) | (jj == cur) | (jj == cur - 1)
    score = jnp.where(forced, 1e6, jnp.where(jj <= cur, imp, -1e6))
    jf = jj.astype(F32)
    sel = jnp.zeros((nq, LANES), F32)
    for _ in range(top_n):
        best = jnp.max(score, axis=-1, keepdims=True)
        idx = jnp.min(jnp.where(score == best, jf, float(LANES)), axis=-1, keepdims=True)
        pick = jf == idx
        sel = jnp.where(pick, 1.0, sel)
        score = jnp.where(pick, -3e38, score)
    sel = sel.astype(BF16)

    m_ref[...] = jnp.full(m_ref.shape, NEG, F32)
    l_ref[...] = jnp.zeros(l_ref.shape, F32)
    acc_ref[...] = jnp.zeros(acc_ref.shape, F32)
    tq_s = t0 + _iota((nq, SEL_KT), 0)
    kt_diag = lax.shift_right_logical(t0, 9)

    def sel_tile(i, carry):
        kt = kt_diag - i
        k0 = pl.multiple_of(kt * SEL_KT, SEL_KT)
        s = _dot_nt(q, ks_ref[pl.ds(k0, SEL_KT), :])
        dist = tq_s - (k0 + _iota((nq, SEL_KT), 1))
        mask = (_dot(sel, e_ref[kt]) > 0.5) & (dist >= 0)
        distf = dist.astype(F32)
        ps, alphas = [], []
        for h in range(NSA_HPG):
            sh = jnp.where(mask, s[hrows[h]] - slope[hrows[h]] * distf, NEG)
            m_prev = m_ref[hrows[h], :]
            m_new = jnp.maximum(m_prev, jnp.max(sh, axis=-1, keepdims=True))
            alpha = jnp.exp(m_prev - m_new)
            p = jnp.exp(sh - m_new[:, 0:1])
            l_ref[hrows[h], :] = alpha * l_ref[hrows[h], :] + jnp.sum(p, axis=-1, keepdims=True)
            m_ref[hrows[h], :] = m_new
            ps.append(p.astype(BF16))
            alphas.append(alpha)
        pv = _dot(jnp.concatenate(ps, axis=0), vs_ref[pl.ds(k0, SEL_KT), :])
        acc_ref[...] = jnp.concatenate(alphas, axis=0) * acc_ref[...] + pv
        return carry

    lax.fori_loop(0, kt_diag + 1, sel_tile, 0)
    o_s = acc_ref[...] / l_ref[...]

    wk = WINDOW + nq
    w0 = pl.multiple_of(jnp.maximum(t0 - WINDOW, 0), nq)
    s_w = _dot_nt(q, kw_ref[pl.ds(w0, wk), :])
    dist_w = (t0 + _iota((nq, wk), 0)) - (w0 + _iota((nq, wk), 1))
    mask_w = (dist_w >= 0) & (dist_w < WINDOW)
    distf_w = dist_w.astype(F32)
    p_w = jnp.concatenate(
        [softmax_rows(s_w[hrows[h]] - slope[hrows[h]] * distf_w, mask_w).astype(BF16)
         for h in range(NSA_HPG)], axis=0)
    o_w = _dot(p_w, vw_ref[pl.ds(w0, wk), :])

    gates = jax.nn.sigmoid(sm_ref[...].astype(F32))

    def gate(branch, h):
        lane = lambda gv: 2 * DN_HEADS + branch * NSA_HEADS + gv * NSA_HPG + h
        cols = [gates[:, lane(gv):lane(gv) + 1] for gv in range(NSA_GROUPS)]
        return jnp.where(grp == 0, cols[0], cols[1])

    for h in range(NSA_HPG):
        o = gate(0, h) * o_c[hrows[h]] + gate(1, h) * o_s[hrows[h]] + gate(2, h) * o_w[hrows[h]]
        o_ref[:, h * HEAD_DIM:(h + 1) * HEAD_DIM] = o.astype(o_ref.dtype)


def _nsa(p3, kc, vc, ks, vs, kw, vw, q_norm):
    b, s, _ = p3.shape
    g = NSA_GROUPS
    ncp = s // CMP_STRIDE
    n_sel = s // SEL_BLOCK
    assert n_sel <= LANES and s % SEL_KT == 0 and s >= WINDOW + Q_BLOCK and NSA_GROUPS == 2
    gw = NSA_HPG * HEAD_DIM
    ci = np.arange(ncp)[:, None]
    sj = np.arange(LANES)[None, :]
    per = SEL_BLOCK // CMP_STRIDE
    overlap = ((ci // per == sj).astype(np.float32) + ((ci + 1) // per == sj).astype(np.float32))
    overlap[ncp - 1:] = 0.0
    key_blk = (np.arange(s) // SEL_BLOCK).reshape(s // SEL_KT, 1, SEL_KT)
    expand = (key_blk == np.arange(LANES)[None, :, None]).astype(np.float32)

    kv_c = pl.BlockSpec((None, None, ncp, HEAD_DIM), lambda i, j, t: (i, j, 0, 0))
    kv_s = pl.BlockSpec((None, None, s, HEAD_DIM), lambda i, j, t: (i, j, 0, 0))
    full = lambda shp: pl.BlockSpec(shp, lambda i, j, t: (0,) * len(shp))
    return pl.pallas_call(
        functools.partial(_nsa_body, top_n=min(SEL_TOPN, n_sel)),
        out_shape=jax.ShapeDtypeStruct((b, s, NSA_WIDTH), BF16),
        grid=(b, g, s // Q_BLOCK),
        in_specs=[pl.BlockSpec((None, Q_BLOCK, gw), lambda i, j, t: (i, t, C_NQ // gw + j)),
                  pl.BlockSpec((None, Q_BLOCK, LANES), lambda i, j, t: (i, t, C_SMALL // LANES)),
                  kv_c, kv_c, kv_s, kv_s, kv_s, kv_s,
                  full((1, HEAD_DIM)), full((ncp, LANES)), full((s // SEL_KT, LANES, SEL_KT))],
        out_specs=pl.BlockSpec((None, Q_BLOCK, gw), lambda i, j, t: (i, t, j)),
        scratch_shapes=[pltpu.VMEM((NSA_HPG * Q_BLOCK, HEAD_DIM), F32)] * 3,
        compiler_params=_cparams(("parallel", "parallel", "arbitrary")),
        name="nsa",
    )(p3, p3, kc, vc, ks, vs, kw, vw, q_norm.reshape(1, -1),
      jnp.asarray(overlap), jnp.asarray(expand, dtype=BF16))


def _layer(x, ffn1_norm, ffn1_gate, ffn1_up, ffn1_down, mix_norm, w_in,
           dn_conv, dn_a_log, dn_dt_bias, dn_out_norm,
           nsa_q_norm, nsa_k_norm, cmpk_pos, cmpk_w1, cmpk_w2, cmpv_pos, cmpv_w1, cmpv_w2,
           sgu_norm_w, sgu_norm_b, sgu_w, sgu_b,
           w_branch_a, w_branch_b, w_branch_c, w_out,
           ffn2_norm, ffn2_gate, ffn2_up, ffn2_down, *, batch):
    t = x.shape[0]
    x = _ffn(x, ffn1_norm, ffn1_gate, ffn1_up, ffn1_down)
    p = _inproj(x, mix_norm, _reorder_w_in(w_in))
    p3 = p.reshape(batch, t // batch, P_COLS)
    y_a = _deltanet(p3, dn_conv, dn_a_log, dn_dt_bias, dn_out_norm)
    kv = _nsa_prep(p3, nsa_k_norm, cmpk_pos, cmpk_w1, cmpk_w2, cmpv_pos, cmpv_w1, cmpv_w2)
    y_b = _nsa(p3, *kv, nsa_q_norm)
    y_c = _sgu(p, sgu_norm_w, sgu_norm_b, sgu_w, sgu_b)
    x = _merge(x, p, y_a.reshape(t, -1), y_b.reshape(t, -1), y_c,
               w_branch_a, w_branch_b, w_branch_c, w_out)
    return _ffn(x, ffn2_norm, ffn2_gate, ffn2_up, ffn2_down)


def kernel(x, ffn1_norm, ffn1_gate, ffn1_up, ffn1_down, mix_norm, w_in, dn_conv, dn_a_log,
           dn_dt_bias, dn_out_norm, nsa_q_norm, nsa_k_norm, cmpk_pos, cmpk_w1, cmpk_w2,
           cmpv_pos, cmpv_w1, cmpv_w2, sgu_norm_w, sgu_norm_b, sgu_w, sgu_b, w_branch_a,
           w_branch_b, w_branch_c, w_out, ffn2_norm, ffn2_gate, ffn2_up, ffn2_down):
    params = (ffn1_norm, ffn1_gate, ffn1_up, ffn1_down, mix_norm, w_in, dn_conv, dn_a_log,
              dn_dt_bias, dn_out_norm, nsa_q_norm, nsa_k_norm, cmpk_pos, cmpk_w1, cmpk_w2,
              cmpv_pos, cmpv_w1, cmpv_w2, sgu_norm_w, sgu_norm_b, sgu_w, sgu_b, w_branch_a,
              w_branch_b, w_branch_c, w_out, ffn2_norm, ffn2_gate, ffn2_up, ffn2_down)
    batch, seq, d = x.shape
    h = x.reshape(batch * seq, d)
    for layer in range(ffn1_norm.shape[0]):
        h = _layer(h, *(w[layer] for w in params), batch=batch)
    return h.reshape(batch, seq, d)
```

```python
import functools

import jax
import jax.numpy as jnp
import numpy as np
from jax import lax
from jax.experimental import pallas as pl
from jax.experimental.pallas import tpu as pltpu

F32 = jnp.float32
BF16 = jnp.bfloat16
HIGHEST = lax.Precision.HIGHEST

LANES = 128
V7X_VMEM_LIMIT = 56 * 1024 * 1024

HEAD_DIM = 128
EPS = 1e-6
NEG = -1e30

DN_HEADS = 8
DN_WIDTH = DN_HEADS * HEAD_DIM
DN_CONV = 4
DN_CHUNK = 64
DN_TILE = 2 * DN_CHUNK
DN_HALO = 8

NSA_HEADS = 8
NSA_GROUPS = 2
NSA_HPG = NSA_HEADS // NSA_GROUPS
NSA_WIDTH = NSA_HEADS * HEAD_DIM
CMP_STRIDE = 16
CMP_BLOCK = 2 * CMP_STRIDE
CMP_HIDDEN = 256
SEL_BLOCK = 64
SEL_TOPN = 16
WINDOW = 512
Q_BLOCK = 128
ALIBI_MAX = 8.0
SEL_KT = 512
POS_SHIFT = 6
POS_SPLIT = 3
LOG2E = 1.4426950408889634

SGU_GROUPS = 8
SGU_WIDTH = SGU_GROUPS * HEAD_DIM
SGU_CHUNK = 128


def _cparams(sem):
    return pltpu.CompilerParams(dimension_semantics=sem, vmem_limit_bytes=V7X_VMEM_LIMIT)


def _rms(x, w):
    return x * lax.rsqrt(jnp.mean(x * x, axis=-1, keepdims=True) + EPS) * w


def _iota(shape, dim):
    return lax.broadcasted_iota(jnp.int32, shape, dim)


def _dot(a, b, precision=None):
    return jnp.dot(a, b, preferred_element_type=F32, precision=precision)


def _dot_nt(a, b, precision=None):
    return lax.dot_general(a, b, (((1,), (1,)), ((), ())), preferred_element_type=F32,
                           precision=precision)


def _ffn_body(x_ref, nw_ref, wg_ref, wu_ref, wd_ref, o_ref, h_ref):
    @pl.when(pl.program_id(1) == 0)
    def _():
        x = x_ref[...]
        h_ref[...] = _rms(x, nw_ref[...]).astype(BF16)
        o_ref[...] = x

    h = h_ref[...]
    g = _dot(h, wg_ref[...])
    u = _dot(h, wu_ref[...])
    a = (0.5 * g * jax.nn.sigmoid(g) * u).astype(BF16)
    o_ref[...] += _dot(a, wd_ref[...])


def _ffn(x, norm_w, wg, wu, wd, *, tm=512, tf=512):
    t, d = x.shape
    ff = wg.shape[1]
    ffp = tf * pl.cdiv(ff, tf)
    wg = jnp.pad(wg.astype(BF16), ((0, 0), (0, ffp - ff)))
    wu = jnp.pad(wu.astype(BF16), ((0, 0), (0, ffp - ff)))
    wd = jnp.pad(wd.astype(BF16), ((0, ffp - ff), (0, 0)))
    return pl.pallas_call(
        _ffn_body,
        out_shape=jax.ShapeDtypeStruct((t, d), F32),
        grid=(t // tm, ffp // tf),
        in_specs=[
            pl.BlockSpec((tm, d), lambda i, j: (i, 0)),
            pl.BlockSpec((1, d), lambda i, j: (0, 0)),
            pl.BlockSpec((d, tf), lambda i, j: (0, j)),
            pl.BlockSpec((d, tf), lambda i, j: (0, j)),
            pl.BlockSpec((tf, d), lambda i, j: (j, 0)),
        ],
        out_specs=pl.BlockSpec((tm, d), lambda i, j: (i, 0)),
        scratch_shapes=[pltpu.VMEM((tm, d), BF16)],
        compiler_params=_cparams(("parallel", "arbitrary")),
        name="ffn",
    )(x, norm_w.reshape(1, d), wg, wu, wd)


C_DQ, C_DK, C_DV, C_DZ = 0, 1024, 2048, 3072
C_NQ = 4096
C_SU, C_SV = 5120, 6144
C_NKV = 7168
C_GA, C_GB, C_GC = 8704, 10752, 12800
C_SMALL = 14848
P_COLS = 15360


def _reorder_w_in(w_in):
    d = w_in.shape[0]
    pad = jnp.zeros((d, P_COLS - 14888), w_in.dtype)
    return jnp.concatenate([
        w_in[:, 0:4096],
        w_in[:, 4112:5136],
        w_in[:, 6696:8744],
        w_in[:, 5136:6672],
        w_in[:, 8744:14888],
        w_in[:, 4096:4112],
        w_in[:, 6672:6696],
        pad], axis=1).astype(BF16)


def _inproj_body(x_ref, nw_ref, w_ref, o_ref, h_ref):
    @pl.when(pl.program_id(1) == 0)
    def _():
        h_ref[...] = _rms(x_ref[...], nw_ref[...]).astype(BF16)

    o_ref[...] = _dot(h_ref[...], w_ref[...]).astype(o_ref.dtype)


def _inproj(x, norm_w, w, *, tm=1024, tn=1024):
    t, d = x.shape
    n = w.shape[1]
    return pl.pallas_call(
        _inproj_body,
        out_shape=jax.ShapeDtypeStruct((t, n), BF16),
        grid=(t // tm, n // tn),
        in_specs=[
            pl.BlockSpec((tm, d), lambda i, j: (i, 0)),
            pl.BlockSpec((1, d), lambda i, j: (0, 0)),
            pl.BlockSpec((d, tn), lambda i, j: (0, j)),
        ],
        out_specs=pl.BlockSpec((tm, tn), lambda i, j: (i, j)),
        scratch_shapes=[pltpu.VMEM((tm, d), BF16)],
        compiler_params=_cparams(("parallel", "arbitrary")),
        name="inproj",
    )(x, norm_w.reshape(1, d), w)


def _merge_body(x_ref, ya_ref, yb_ref, yc_ref, ga_ref, gb_ref, gc_ref,
                wa_ref, wb_ref, wc_ref, wo_ref, o_ref):
    @pl.when(pl.program_id(1) == 0)
    def _():
        o_ref[...] = x_ref[...]

    m = jax.nn.sigmoid(ga_ref[...].astype(F32)) * _dot(ya_ref[...], wa_ref[...])
    m += jax.nn.sigmoid(gb_ref[...].astype(F32)) * _dot(yb_ref[...], wb_ref[...])
    m += jax.nn.sigmoid(gc_ref[...].astype(F32)) * _dot(yc_ref[...], wc_ref[...])
    o_ref[...] += _dot(m.astype(BF16), wo_ref[...])


def _merge(x, p, ya, yb, yc, wa, wb, wc, wo, *, tm=512, tn=512):
    t, d = x.shape
    k = ya.shape[1]
    y_spec = pl.BlockSpec((tm, k), lambda i, j: (i, 0))
    w_spec = pl.BlockSpec((k, tn), lambda i, j: (0, j))

    def gate_spec(col):
        return pl.BlockSpec((tm, tn), lambda i, j: (i, col // tn + j))

    return pl.pallas_call(
        _merge_body,
        out_shape=jax.ShapeDtypeStruct((t, d), F32),
        grid=(t // tm, d // tn),
        in_specs=[pl.BlockSpec((tm, d), lambda i, j: (i, 0)), y_spec, y_spec, y_spec,
                  gate_spec(C_GA), gate_spec(C_GB), gate_spec(C_GC),
                  w_spec, w_spec, w_spec,
                  pl.BlockSpec((tn, d), lambda i, j: (j, 0))],
        out_specs=pl.BlockSpec((tm, d), lambda i, j: (i, 0)),
        compiler_params=_cparams(("parallel", "arbitrary")),
        name="merge",
    )(x, ya, yb, yc, p, p, p, wa.astype(BF16), wb.astype(BF16), wc.astype(BF16), wo.astype(BF16))


def _sgu_body(u_ref, v_ref, nw_ref, nb_ref, ws_ref, bs_ref, o_ref, *, n_chunk):
    v = jax.nn.gelu(v_ref[...].astype(F32))
    mu = jnp.mean(v, axis=-1, keepdims=True)
    vc = v - mu
    var = jnp.mean(vc * vc, axis=-1, keepdims=True)
    vn = (vc * lax.rsqrt(var + EPS) * nw_ref[...] + nb_ref[...]).astype(BF16)
    tril = _iota((SGU_CHUNK, SGU_CHUNK), 0) >= _iota((SGU_CHUNK, SGU_CHUNK), 1)
    for g in range(SGU_GROUPS):
        cols = slice(g * HEAD_DIM, (g + 1) * HEAD_DIM)
        w = jnp.where(tril, ws_ref[g], 0.0).astype(BF16)
        rhs = jnp.concatenate(
            [vn[c * SGU_CHUNK:(c + 1) * SGU_CHUNK, cols] for c in range(n_chunk)], axis=1)
        s = _dot(w, rhs)
        for c in range(n_chunk):
            rows = slice(c * SGU_CHUNK, (c + 1) * SGU_CHUNK)
            u = jax.nn.gelu(u_ref[rows, cols].astype(F32))
            sc = s[:, c * HEAD_DIM:(c + 1) * HEAD_DIM] + bs_ref[g]
            o_ref[rows, cols] = (u * sc).astype(o_ref.dtype)


def _sgu(p, norm_w, norm_b, w_s, b_s, *, n_chunk=4):
    t = p.shape[0]
    ts = n_chunk * SGU_CHUNK
    bias = jnp.broadcast_to(b_s[:, :, None], (SGU_GROUPS, SGU_CHUNK, HEAD_DIM))
    return pl.pallas_call(
        functools.partial(_sgu_body, n_chunk=n_chunk),
        out_shape=jax.ShapeDtypeStruct((t, SGU_WIDTH), BF16),
        grid=(t // ts,),
        in_specs=[
            pl.BlockSpec((ts, SGU_WIDTH), lambda i: (i, C_SU // SGU_WIDTH)),
            pl.BlockSpec((ts, SGU_WIDTH), lambda i: (i, C_SV // SGU_WIDTH)),
            pl.BlockSpec((1, SGU_WIDTH), lambda i: (0, 0)),
            pl.BlockSpec((1, SGU_WIDTH), lambda i: (0, 0)),
            pl.BlockSpec((SGU_GROUPS, SGU_CHUNK, SGU_CHUNK), lambda i: (0, 0, 0)),
            pl.BlockSpec((SGU_GROUPS, SGU_CHUNK, HEAD_DIM), lambda i: (0, 0, 0)),
        ],
        out_specs=pl.BlockSpec((ts, SGU_WIDTH), lambda i: (i, 0)),
        compiler_params=_cparams(("parallel",)),
        name="sgu",
    )(p, p, norm_w.reshape(1, -1), norm_b.reshape(1, -1), w_s, bias)


def _softplus(x):
    return jnp.maximum(x, 0.0) + jnp.log(1.0 + jnp.exp(-jnp.abs(x)))


def _dn_body(q_ref, k_ref, v_ref, z_ref, sm_ref, conv_ref, alog_ref, dtb_ref, onorm_ref,
             o_ref, xbuf, s_ref):
    n = DN_TILE
    shape = (n, n)

    @pl.when(pl.program_id(1) == 0)
    def _():
        xbuf[0:DN_HALO, :] = jnp.zeros((DN_HALO, 3 * DN_WIDTH), F32)
        s_ref[...] = jnp.zeros_like(s_ref)

    xbuf[DN_HALO:, 0:DN_WIDTH] = q_ref[...].astype(F32)
    xbuf[DN_HALO:, DN_WIDTH:2 * DN_WIDTH] = k_ref[...].astype(F32)
    xbuf[DN_HALO:, 2 * DN_WIDTH:] = v_ref[...].astype(F32)

    def conv_silu(c0):
        cols = slice(c0, c0 + HEAD_DIM)
        acc = conv_ref[DN_CONV - 1:DN_CONV, cols] * xbuf[DN_HALO:DN_HALO + n, cols]
        for s in range(1, DN_CONV):
            acc += (conv_ref[DN_CONV - 1 - s:DN_CONV - s, cols]
                    * xbuf[DN_HALO - s:DN_HALO - s + n, cols])
        return acc * jax.nn.sigmoid(acc)

    row = _iota(shape, 0)
    col = _iota(shape, 1)
    same = lax.shift_right_logical(row, 6) == lax.shift_right_logical(col, 6)
    incl = same & (row >= col)
    strict = same & (row > col)
    first_chunk_cols = col < DN_CHUNK
    eye = jnp.where(row == col, 1.0, 0.0).astype(BF16)

    sm = sm_ref[...].astype(F32)
    g_all = -jnp.exp(alog_ref[...]) * _softplus(sm + dtb_ref[...])
    beta_all = jax.nn.sigmoid(sm)
    tri = jnp.where(incl, 1.0, 0.0)
    gc_all = _dot(tri, g_all, HIGHEST)
    last = jnp.where(col == jnp.where(row >= DN_CHUNK, n - 1, DN_CHUNK - 1), 1.0, 0.0)
    gl_all = _dot(last, gc_all, HIGHEST)
    gc_t = gc_all.T

    heads = range(DN_HEADS)
    c = DN_CHUNK
    lows, rhss, attns, q_decs, kd_t0s, kd_t1s, ends0, ends1 = [], [], [], [], [], [], [], []
    for h in heads:
        qa = conv_silu(h * HEAD_DIM)
        ka = conv_silu(DN_WIDTH + h * HEAD_DIM)
        va = conv_silu(2 * DN_WIDTH + h * HEAD_DIM)
        qn = qa * lax.rsqrt(jnp.sum(qa * qa, axis=-1, keepdims=True) + EPS) * HEAD_DIM ** -0.5
        kn = ka * lax.rsqrt(jnp.sum(ka * ka, axis=-1, keepdims=True) + EPS)
        beta = jnp.broadcast_to(beta_all[:, DN_HEADS + h:DN_HEADS + h + 1], shape)
        g_col = jnp.broadcast_to(gc_all[:, h:h + 1], shape)
        g_row = jnp.broadcast_to(gc_t[h:h + 1, :], shape)
        g_end = jnp.broadcast_to(gl_all[:, h:h + 1], shape)
        decay = jnp.where(incl, jnp.exp(jnp.where(incl, g_col - g_row, 0.0)), 0.0)
        e_col = jnp.exp(g_col)
        knb = kn.astype(BF16)
        lows.append(jnp.where(strict, beta * _dot_nt(knb, knb) * decay, 0.0))
        attns.append((_dot_nt(qn.astype(BF16), knb) * decay).astype(BF16))
        rhss.append(jnp.concatenate([va * beta, kn * beta * e_col], axis=1))
        q_decs.append((qn * e_col).astype(BF16))
        kd_t = _dot_nt(eye, (kn * jnp.exp(g_end - g_col)).astype(BF16))
        kd_t0s.append(jnp.where(first_chunk_cols, kd_t, 0.0).astype(BF16))
        kd_t1s.append(jnp.where(first_chunk_cols, 0.0, kd_t).astype(BF16))
        decay_end = jnp.exp(g_end)
        ends0.append(decay_end[0:1, :])
        ends1.append(decay_end[c:c + 1, :])

    ms = [(-low).astype(BF16) for low in lows]
    xs = [rhs + _dot(m, rhs.astype(BF16)) for m, rhs in zip(ms, rhss)]
    for _ in range(5):
        ms = [_dot(m, m).astype(BF16) for m in ms]
        xs = [x + _dot(m, x.astype(BF16)) for m, x in zip(ms, xs)]
    us = [x[:, :HEAD_DIM] for x in xs]
    ws = [x[:, HEAD_DIM:].astype(BF16) for x in xs]

    s0 = [s_ref[h] for h in heads]
    s0b = [s.astype(BF16) for s in s0]
    v0 = [us[h][:c] - _dot(ws[h][:c], s0b[h]) for h in heads]
    o0 = [_dot(q_decs[h][:c], s0b[h]) for h in heads]
    vf0 = [jnp.concatenate([v, jnp.zeros_like(v)], axis=0).astype(BF16) for v in v0]
    s1 = [s0[h] * ends0[h] + _dot(kd_t0s[h], vf0[h]) for h in heads]
    s1b = [s.astype(BF16) for s in s1]
    v1 = [us[h][c:] - _dot(ws[h][c:], s1b[h]) for h in heads]
    o1 = [_dot(q_decs[h][c:], s1b[h]) for h in heads]
    vf = [jnp.concatenate([v0[h], v1[h]], axis=0).astype(BF16) for h in heads]
    for h in heads:
        s_ref[h] = s1[h] * ends1[h] + _dot(kd_t1s[h], vf[h])
    for h in heads:
        hc = slice(h * HEAD_DIM, (h + 1) * HEAD_DIM)
        o = jnp.concatenate([o0[h], o1[h]], axis=0) + _dot(attns[h], vf[h])
        z = z_ref[:, hc].astype(F32)
        o_ref[:, hc] = (_rms(o, onorm_ref[...]) * (z * jax.nn.sigmoid(z))).astype(o_ref.dtype)

    xbuf[0:DN_HALO, :] = xbuf[n:n + DN_HALO, :]


def _deltanet(p3, conv_w, a_log, dt_bias, out_norm):
    b, s, _ = p3.shape
    wide = lambda c: pl.BlockSpec((None, DN_TILE, DN_WIDTH), lambda i, t: (i, t, c // DN_WIDTH))
    lane_pad = lambda a: jnp.pad(a.reshape(1, -1), ((0, 0), (0, LANES - a.shape[-1])))
    full = lambda shp: pl.BlockSpec(shp, lambda i, t: (0,) * len(shp))
    return pl.pallas_call(
        _dn_body,
        out_shape=jax.ShapeDtypeStruct((b, s, DN_WIDTH), BF16),
        grid=(b, s // DN_TILE),
        in_specs=[wide(C_DQ), wide(C_DK), wide(C_DV), wide(C_DZ),
                  pl.BlockSpec((None, DN_TILE, LANES), lambda i, t: (i, t, C_SMALL // LANES)),
                  full((DN_CONV, 3 * DN_WIDTH)), full((1, LANES)), full((1, LANES)),
                  full((1, HEAD_DIM))],
        out_specs=pl.BlockSpec((None, DN_TILE, DN_WIDTH), lambda i, t: (i, t, 0)),
        scratch_shapes=[pltpu.VMEM((DN_HALO + DN_TILE, 3 * DN_WIDTH), F32),
                        pltpu.VMEM((DN_HEADS, HEAD_DIM, HEAD_DIM), F32)],
        compiler_params=_cparams(("parallel", "arbitrary")),
        name="deltanet",
    )(p3, p3, p3, p3, p3, conv_w, lane_pad(a_log), lane_pad(dt_bias), out_norm.reshape(1, -1))


def _pos_features(n, stride, offset):
    pos = _iota((n, LANES), 0) * stride + offset
    lane = _iota((n, LANES), 1)
    feat = jnp.where(lane < POS_SPLIT, lax.shift_right_logical(pos, POS_SHIFT),
                     jnp.where(lane < 2 * POS_SPLIT, pos & ((1 << POS_SHIFT) - 1), 0))
    return feat.astype(F32).astype(BF16)


def _nsa_keys_body(ks_ref, vs_ref, kw_ref, vw_ref, knorm_ref, ks_o, vs_o, kw_o, vw_o):
    rows = ks_o.shape[0]
    feat = _pos_features(rows, 1, pl.program_id(2) * rows)
    ks_o[:, :HEAD_DIM] = _rms(ks_ref[...].astype(F32), knorm_ref[1:2, :]).astype(BF16)
    ks_o[:, HEAD_DIM:] = feat
    vs_o[...] = vs_ref[...]
    kw_o[:, :HEAD_DIM] = _rms(kw_ref[...].astype(F32), knorm_ref[2:3, :]).astype(BF16)
    kw_o[:, HEAD_DIM:] = feat
    vw_o[...] = vw_ref[...]


def _nsa_cmp_body(yk_ref, yv_ref, kpos_ref, kw1_ref, kw2_ref, vpos_ref, vw1_ref, vw2_ref,
                  knorm_ref, kc_o, vc_o):
    def compress(y_ref, pos_ref, w1_ref, w2_ref):
        y = y_ref[...].astype(F32)
        a = _dot((y + pos_ref[0:1, :]).astype(BF16), w1_ref[0])
        b = _dot((y + pos_ref[1:2, :]).astype(BF16), w1_ref[1])
        hid = a + pltpu.roll(b, b.shape[0] - 1, axis=0)
        hid = hid * jax.nn.sigmoid(hid)
        return _dot(hid.astype(BF16), w2_ref[...])

    n_sub = kc_o.shape[0]
    kc_o[:, :HEAD_DIM] = _rms(compress(yk_ref, kpos_ref, kw1_ref, kw2_ref),
                              knorm_ref[0:1, :]).astype(BF16)
    kc_o[:, HEAD_DIM:] = _pos_features(n_sub, CMP_STRIDE, CMP_BLOCK - 1)
    vc_o[...] = compress(yv_ref, vpos_ref, vw1_ref, vw2_ref).astype(BF16)


def _nsa_prep(p3, k_norm, cmpk_pos, cmpk_w1, cmpk_w2, cmpv_pos, cmpv_w1, cmpv_w2, *, rows=2048):
    b, s, _ = p3.shape
    g = NSA_GROUPS
    n_sub = s // CMP_STRIDE
    half = CMP_STRIDE * HEAD_DIM
    kv = g * HEAD_DIM
    k_aug = 2 * HEAD_DIM

    def stride_chunks(c0):
        y = p3[:, :, c0:c0 + kv].reshape(b, n_sub, CMP_STRIDE, g, HEAD_DIM)
        return jnp.transpose(y, (0, 3, 1, 2, 4)).reshape(b, g, n_sub, half)

    full2 = lambda shp: pl.BlockSpec(shp, lambda i, j: (0,) * len(shp))
    cmp_io = lambda w: pl.BlockSpec((None, None, n_sub, w), lambda i, j: (i, j, 0, 0))
    cmp_shape = lambda w: jax.ShapeDtypeStruct((b, g, n_sub, w), BF16)
    kc, vc = pl.pallas_call(
        _nsa_cmp_body,
        out_shape=(cmp_shape(k_aug), cmp_shape(HEAD_DIM)),
        grid=(b, g),
        in_specs=[cmp_io(half), cmp_io(half),
                  full2((2, half)), full2((2, half, CMP_HIDDEN)), full2((CMP_HIDDEN, HEAD_DIM)),
                  full2((2, half)), full2((2, half, CMP_HIDDEN)), full2((CMP_HIDDEN, HEAD_DIM)),
                  full2((3, HEAD_DIM))],
        out_specs=(cmp_io(k_aug), cmp_io(HEAD_DIM)),
        compiler_params=_cparams(("parallel", "parallel")),
        name="nsa_cmp",
    )(stride_chunks(C_NKV), stride_chunks(C_NKV + kv),
      cmpk_pos.reshape(2, half), cmpk_w1.reshape(2, half, CMP_HIDDEN).astype(BF16),
      cmpk_w2.astype(BF16),
      cmpv_pos.reshape(2, half), cmpv_w1.reshape(2, half, CMP_HIDDEN).astype(BF16),
      cmpv_w2.astype(BF16), k_norm)

    rows = min(rows, s)
    col = lambda c0: pl.BlockSpec((None, rows, HEAD_DIM),
                                  lambda i, j, t: (i, t, c0 // HEAD_DIM + j))
    key_io = lambda w: pl.BlockSpec((None, None, rows, w), lambda i, j, t: (i, j, t, 0))
    key_shape = lambda w: jax.ShapeDtypeStruct((b, g, s, w), BF16)
    ks, vs, kw, vw = pl.pallas_call(
        _nsa_keys_body,
        out_shape=(key_shape(k_aug), key_shape(HEAD_DIM), key_shape(k_aug), key_shape(HEAD_DIM)),
        grid=(b, g, s // rows),
        in_specs=[col(C_NKV + 2 * kv), col(C_NKV + 3 * kv), col(C_NKV + 4 * kv), col(C_NKV + 5 * kv),
                  pl.BlockSpec((3, HEAD_DIM), lambda i, j, t: (0, 0))],
        out_specs=(key_io(k_aug), key_io(HEAD_DIM), key_io(k_aug), key_io(HEAD_DIM)),
        compiler_params=_cparams(("parallel", "parallel", "parallel")),
        name="nsa_keys",
    )(p3, p3, p3, p3, k_norm)
    return kc, vc, ks, vs, kw, vw


def _split3(x):
    hi = x.astype(BF16).astype(F32)
    mid = (x - hi).astype(BF16).astype(F32)
    return hi, mid, x - hi - mid


def _nsa_body(q_ref, sm_ref, kc_ref, vc_ref, ks_ref, vs_ref, kw_ref, vw_ref,
              qn_ref, ov_ref, e_ref, tg_ref, o_ref,
              m_ref, l_ref, acc_ref, part_ref, flag_ref, *, top_n):
    grp = pl.program_id(1)
    qb = pl.program_id(2)
    t0 = qb * Q_BLOCK
    nq = Q_BLOCK
    nr = NSA_HPG * nq
    ncp = kc_ref.shape[0]
    hrows = [slice(h * nq, (h + 1) * nq) for h in range(NSA_HPG)]
    hcols = [slice(h * HEAD_DIM, (h + 1) * HEAD_DIM) for h in range(NSA_HPG)]

    head = (lax.shift_right_logical(_iota((nr, 1), 0), 7) + grp * NSA_HPG + 1).astype(F32)
    slope = jnp.exp(head * (-ALIBI_MAX / NSA_HEADS * np.log(2.0))) * LOG2E
    lane = _iota((nr, LANES), 1)
    slope_feat = jnp.zeros((nr, LANES), F32)
    for i, term in enumerate(_split3(slope)):
        slope_feat = jnp.where(lane == i, term * float(1 << POS_SHIFT), slope_feat)
        slope_feat = jnp.where(lane == POS_SPLIT + i, term, slope_feat)
    qh = jnp.concatenate(
        [_rms(q_ref[:, hcols[h]].astype(F32), qn_ref[...]) * (HEAD_DIM ** -0.5 * LOG2E)
         for h in range(NSA_HPG)], axis=0)
    q = jnp.concatenate([qh, slope_feat], axis=1).astype(BF16)

    def softmax_rows(s, mask):
        s = jnp.where(mask, s, NEG)
        e = jnp.exp2(s - jnp.max(s, axis=-1, keepdims=True))
        return jnp.where(mask, e / jnp.sum(e, axis=-1, keepdims=True), 0.0)

    gates = jax.nn.sigmoid(sm_ref[...].astype(F32))

    def gate(branch, h):
        at = lambda gv: 2 * DN_HEADS + branch * NSA_HEADS + gv * NSA_HPG + h
        cols = [gates[:, at(gv):at(gv) + 1] for gv in range(NSA_GROUPS)]
        return jnp.where(grp == 0, cols[0], cols[1])

    s_c = _dot_nt(q, kc_ref[...])
    mask_c = (t0 + _iota((nq, ncp), 0)) >= (_iota((nq, ncp), 1) * CMP_STRIDE + (CMP_BLOCK - 1))
    p_c = [softmax_rows(s_c[hrows[h]], mask_c) for h in range(NSA_HPG)]
    o_c = _dot(jnp.concatenate([p.astype(BF16) for p in p_c], axis=0), vc_ref[...])

    p_grp = p_c[0]
    for h in range(1, NSA_HPG):
        p_grp = p_grp + p_c[h]
    imp = _dot(p_grp, ov_ref[...], HIGHEST)
    jj = _iota((nq, LANES), 1)
    cur = lax.shift_right_logical(t0 + _iota((nq, LANES), 0), 6)
    forced = (jj == 0) | (jj == cur) | (jj == cur - 1)
    score = jnp.where(forced, -3e38, jnp.where(jj <= cur, imp, -1e6))
    sel = jnp.where(forced, 1.0, 0.0)

    wk = WINDOW + nq
    w0 = pl.multiple_of(jnp.maximum(t0 - WINDOW, 0), nq)
    s_w = _dot_nt(q, kw_ref[pl.ds(w0, wk), :])
    dist_w = (t0 + _iota((nq, wk), 0)) - (w0 + _iota((nq, wk), 1))
    mask_w = (dist_w >= 0) & (dist_w < WINDOW)
    p_w = []
    rounds = top_n - 3
    for r in range(rounds):
        pick = jj == jnp.argmax(score, axis=-1, keepdims=True)
        sel = jnp.where(pick, 1.0, sel)
        score = jnp.where(pick, -3e38, score)
        if (r + 1) * NSA_HPG // rounds > len(p_w):
            p_w.append(softmax_rows(s_w[hrows[len(p_w)]], mask_w).astype(BF16))
    sel = sel.astype(BF16)
    o_w = _dot(jnp.concatenate(p_w, axis=0), vw_ref[pl.ds(w0, wk), :])
    for h in range(NSA_HPG):
        part_ref[:, hcols[h]] = gate(0, h) * o_c[hrows[h]] + gate(2, h) * o_w[hrows[h]]

    picked = jnp.broadcast_to(jnp.sum(sel.astype(F32), axis=0, keepdims=True), (8, LANES))
    tile_count = _dot(picked.astype(BF16), tg_ref[...])
    for t in range(flag_ref.shape[0]):
        flag_ref[t] = (tile_count[0, t] > 0.0).astype(jnp.int32)

    m_ref[...] = jnp.full(m_ref.shape, NEG, F32)
    l_ref[...] = jnp.zeros(l_ref.shape, F32)
    acc_ref[...] = jnp.zeros(acc_ref.shape, F32)
    kt_diag = lax.shift_right_logical(t0, 9)

    def sel_tile(kt, causal):
        k0 = pl.multiple_of(kt * SEL_KT, SEL_KT)
        s = _dot_nt(q, ks_ref[pl.ds(k0, SEL_KT), :])
        mask = _dot(sel, e_ref[kt]) > 0.5
        if causal:
            mask = mask & ((t0 + _iota((nq, SEL_KT), 0)) >= (k0 + _iota((nq, SEL_KT), 1)))
        ps, alphas = [], []
        for h in range(NSA_HPG):
            sh = jnp.where(mask, s[hrows[h]], NEG)
            m_prev = m_ref[hrows[h], :]
            m_new = jnp.maximum(m_prev, jnp.max(sh, axis=-1, keepdims=True))
            alpha = jnp.exp2(m_prev - m_new)
            p = jnp.exp2(sh - m_new[:, 0:1])
            l_ref[hrows[h], :] = alpha * l_ref[hrows[h], :] + jnp.sum(p, axis=-1, keepdims=True)
            m_ref[hrows[h], :] = m_new
            ps.append(p.astype(BF16))
            alphas.append(alpha)
        pv = _dot(jnp.concatenate(ps, axis=0), vs_ref[pl.ds(k0, SEL_KT), :])
        acc_ref[...] = jnp.concatenate(alphas, axis=0) * acc_ref[...] + pv

    sel_tile(kt_diag, True)

    def earlier_tile(i, carry):
        kt = kt_diag - 1 - i

        @pl.when(flag_ref[kt] > 0)
        def _():
            sel_tile(kt, False)

        return carry

    lax.fori_loop(0, kt_diag, earlier_tile, 0)
    o_s = acc_ref[...] / l_ref[...]
    gates = jax.nn.sigmoid(sm_ref[...].astype(F32))
    for h in range(NSA_HPG):
        o = part_ref[:, hcols[h]] + gate(1, h) * o_s[hrows[h]]
        o_ref[:, hcols[h]] = o.astype(o_ref.dtype)


def _nsa(p3, kc, vc, ks, vs, kw, vw, q_norm):
    b, s, _ = p3.shape
    g = NSA_GROUPS
    ncp = s // CMP_STRIDE
    n_sel = s // SEL_BLOCK
    assert n_sel <= LANES and s % SEL_KT == 0 and s >= WINDOW + Q_BLOCK and NSA_GROUPS == 2
    assert min(SEL_TOPN, n_sel) - 3 >= NSA_HPG
    gw = NSA_HPG * HEAD_DIM
    ci = np.arange(ncp)[:, None]
    sj = np.arange(LANES)[None, :]
    per = SEL_BLOCK // CMP_STRIDE
    overlap = ((ci // per == sj).astype(np.float32) + ((ci + 1) // per == sj).astype(np.float32))
    overlap[ncp - 1:] = 0.0
    key_blk = (np.arange(s) // SEL_BLOCK).reshape(s // SEL_KT, 1, SEL_KT)
    expand = (key_blk == np.arange(LANES)[None, :, None]).astype(np.float32)

    n_tiles = s // SEL_KT
    tile_of = (np.arange(LANES)[:, None] // (SEL_KT // SEL_BLOCK)
               == np.arange(LANES)[None, :]).astype(np.float32)

    kv = lambda rows, w: pl.BlockSpec((None, None, rows, w), lambda i, j, t: (i, j, 0, 0))
    full = lambda shp: pl.BlockSpec(shp, lambda i, j, t: (0,) * len(shp))
    return pl.pallas_call(
        functools.partial(_nsa_body, top_n=min(SEL_TOPN, n_sel)),
        out_shape=jax.ShapeDtypeStruct((b, s, NSA_WIDTH), BF16),
        grid=(b, g, s // Q_BLOCK),
        in_specs=[pl.BlockSpec((None, Q_BLOCK, gw), lambda i, j, t: (i, t, C_NQ // gw + j)),
                  pl.BlockSpec((None, Q_BLOCK, LANES), lambda i, j, t: (i, t, C_SMALL // LANES)),
                  kv(ncp, 2 * HEAD_DIM), kv(ncp, HEAD_DIM), kv(s, 2 * HEAD_DIM), kv(s, HEAD_DIM),
                  kv(s, 2 * HEAD_DIM), kv(s, HEAD_DIM),
                  full((1, HEAD_DIM)), full((ncp, LANES)), full((n_tiles, LANES, SEL_KT)),
                  full((LANES, LANES))],
        out_specs=pl.BlockSpec((None, Q_BLOCK, gw), lambda i, j, t: (i, t, j)),
        scratch_shapes=[pltpu.VMEM((NSA_HPG * Q_BLOCK, HEAD_DIM), F32)] * 3
        + [pltpu.VMEM((Q_BLOCK, gw), F32), pltpu.SMEM((n_tiles,), jnp.int32)],
        compiler_params=_cparams(("parallel", "parallel", "arbitrary")),
        name="nsa",
    )(p3, p3, kc, vc, ks, vs, kw, vw, q_norm.reshape(1, -1),
      jnp.asarray(overlap), jnp.asarray(expand, dtype=BF16), jnp.asarray(tile_of, dtype=BF16))


def _layer(x, ffn1_norm, ffn1_gate, ffn1_up, ffn1_down, mix_norm, w_in,
           dn_conv, dn_a_log, dn_dt_bias, dn_out_norm,
           nsa_q_norm, nsa_k_norm, cmpk_pos, cmpk_w1, cmpk_w2, cmpv_pos, cmpv_w1, cmpv_w2,
           sgu_norm_w, sgu_norm_b, sgu_w, sgu_b,
           w_branch_a, w_branch_b, w_branch_c, w_out,
           ffn2_norm, ffn2_gate, ffn2_up, ffn2_down, *, batch):
    t = x.shape[0]
    x = _ffn(x, ffn1_norm, ffn1_gate, ffn1_up, ffn1_down)
    p = _inproj(x, mix_norm, _reorder_w_in(w_in))
    p3 = p.reshape(batch, t // batch, P_COLS)
    y_a = _deltanet(p3, dn_conv, dn_a_log, dn_dt_bias, dn_out_norm)
    kv = _nsa_prep(p3, nsa_k_norm, cmpk_pos, cmpk_w1, cmpk_w2, cmpv_pos, cmpv_w1, cmpv_w2)
    y_b = _nsa(p3, *kv, nsa_q_norm)
    y_c = _sgu(p, sgu_norm_w, sgu_norm_b, sgu_w, sgu_b)
    x = _merge(x, p, y_a.reshape(t, -1), y_b.reshape(t, -1), y_c,
               w_branch_a, w_branch_b, w_branch_c, w_out)
    return _ffn(x, ffn2_norm, ffn2_gate, ffn2_up, ffn2_down)


def kernel(x, ffn1_norm, ffn1_gate, ffn1_up, ffn1_down, mix_norm, w_in, dn_conv, dn_a_log,
           dn_dt_bias, dn_out_norm, nsa_q_norm, nsa_k_norm, cmpk_pos, cmpk_w1, cmpk_w2,
           cmpv_pos, cmpv_w1, cmpv_w2, sgu_norm_w, sgu_norm_b, sgu_w, sgu_b, w_branch_a,
           w_branch_b, w_branch_c, w_out, ffn2_norm, ffn2_gate, ffn2_up, ffn2_down):
    params = (ffn1_norm, ffn1_gate, ffn1_up, ffn1_down, mix_norm, w_in, dn_conv, dn_a_log,
              dn_dt_bias, dn_out_norm, nsa_q_norm, nsa_k_norm, cmpk_pos, cmpk_w1, cmpk_w2,
              cmpv_pos, cmpv_w1, cmpv_w2, sgu_norm_w, sgu_norm_b, sgu_w, sgu_b, w_branch_a,
              w_branch_b, w_branch_c, w_out, ffn2_norm, ffn2_gate, ffn2_up, ffn2_down)
    batch, seq, d = x.shape
    h = x.reshape(batch * seq, d)
    for layer in range(ffn1_norm.shape[0]):
        h = _layer(h, *(w[layer] for w in params), batch=batch)
    return h.reshape(batch, seq, d)
```

```python
import functools

import jax
import jax.numpy as jnp
import numpy as np
from jax import lax
from jax.experimental import pallas as pl
from jax.experimental.pallas import tpu as pltpu

F32 = jnp.float32
BF16 = jnp.bfloat16
HIGHEST = lax.Precision.HIGHEST

LANES = 128
V7X_VMEM_LIMIT = 56 * 1024 * 1024

HEAD_DIM = 128
EPS = 1e-6
NEG = -1e30

DN_HEADS = 8
DN_WIDTH = DN_HEADS * HEAD_DIM
DN_CONV = 4
DN_CHUNK = 64
DN_TILE = 2 * DN_CHUNK
DN_HALO = 8

NSA_HEADS = 8
NSA_GROUPS = 2
NSA_HPG = NSA_HEADS // NSA_GROUPS
NSA_WIDTH = NSA_HEADS * HEAD_DIM
CMP_STRIDE = 16
CMP_BLOCK = 2 * CMP_STRIDE
CMP_HIDDEN = 256
SEL_BLOCK = 64
SEL_TOPN = 16
WINDOW = 512
Q_SHIFT = 8
Q_TILE = 1 << Q_SHIFT
ALIBI_MAX = 8.0
SEL_KT = 512
POS_SHIFT = 6
POS_SPLIT = 3
LOG2E = 1.4426950408889634

SGU_GROUPS = 8
SGU_WIDTH = SGU_GROUPS * HEAD_DIM
SGU_CHUNK = 128


def _cparams(sem):
    return pltpu.CompilerParams(dimension_semantics=sem, vmem_limit_bytes=V7X_VMEM_LIMIT)


def _rms(x, w):
    return x * lax.rsqrt(jnp.mean(x * x, axis=-1, keepdims=True) + EPS) * w


def _iota(shape, dim):
    return lax.broadcasted_iota(jnp.int32, shape, dim)


def _dot(a, b, precision=None):
    return jnp.dot(a, b, preferred_element_type=F32, precision=precision)


def _dot_nt(a, b, precision=None):
    return lax.dot_general(a, b, (((1,), (1,)), ((), ())), preferred_element_type=F32,
                           precision=precision)


def _ffn_body(x_ref, nw_ref, wg_ref, wu_ref, wd_ref, o_ref, h_ref):
    @pl.when(pl.program_id(1) == 0)
    def _():
        x = x_ref[...]
        h_ref[...] = _rms(x, nw_ref[...]).astype(BF16)
        o_ref[...] = x

    h = h_ref[...]
    g = _dot(h, wg_ref[...])
    u = _dot(h, wu_ref[...])
    a = (0.5 * g * jax.nn.sigmoid(g) * u).astype(BF16)
    o_ref[...] += _dot(a, wd_ref[...])


FF_TILE = 512


def _ffn_weights(w_gate, w_up, w_down):
    ff = w_gate.shape[-1]
    extra = FF_TILE * pl.cdiv(ff, FF_TILE) - ff
    cols = lambda w: jnp.pad(w, ((0, 0), (0, 0), (0, extra))).astype(BF16)
    return cols(w_gate), cols(w_up), jnp.pad(w_down, ((0, 0), (0, extra), (0, 0))).astype(BF16)


def _ffn(x, norm_w, wg, wu, wd, layer, *, tm=1024, tf=FF_TILE):
    t, d = x.shape
    ffp = wg.shape[-1]
    return pl.pallas_call(
        _ffn_body,
        out_shape=jax.ShapeDtypeStruct((t, d), F32),
        grid=(t // tm, ffp // tf),
        in_specs=[
            pl.BlockSpec((tm, d), lambda i, j: (i, 0)),
            pl.BlockSpec((1, d), lambda i, j: (0, 0)),
            pl.BlockSpec((None, d, tf), lambda i, j: (layer, 0, j)),
            pl.BlockSpec((None, d, tf), lambda i, j: (layer, 0, j)),
            pl.BlockSpec((None, tf, d), lambda i, j: (layer, j, 0)),
        ],
        out_specs=pl.BlockSpec((tm, d), lambda i, j: (i, 0)),
        scratch_shapes=[pltpu.VMEM((tm, d), BF16)],
        compiler_params=_cparams(("parallel", "arbitrary")),
        name="ffn",
    )(x, norm_w.reshape(1, d), wg, wu, wd)


C_DQ, C_DK, C_DV, C_DZ = 0, 1024, 2048, 3072
C_NQ = 4096
C_SU, C_SV = 5120, 6144
C_GA, C_GB, C_GC = 7168, 9216, 11264
C_NKV = 13312
C_SMALL = 14848
P_COLS = 15360


def _reorder_w_in(w_in):
    w_in = w_in.astype(BF16)
    pad = jnp.zeros(w_in.shape[:-1] + (P_COLS - w_in.shape[-1],), BF16)
    return jnp.concatenate([
        w_in[..., 0:4096],
        w_in[..., 4112:5136],
        w_in[..., 6696:8744],
        w_in[..., 8744:14888],
        w_in[..., 5136:6672],
        w_in[..., 4096:4112],
        w_in[..., 6672:6696],
        pad], axis=-1)


def _inproj_body(x_ref, nw_ref, w_ref, o_ref, h_ref):
    @pl.when(pl.program_id(1) == 0)
    def _():
        h_ref[...] = _rms(x_ref[...], nw_ref[...]).astype(BF16)

    o_ref[...] = _dot(h_ref[...], w_ref[...]).astype(o_ref.dtype)


def _inproj(x, norm_w, w, layer, *, tm=1024, tn=1024):
    t, d = x.shape
    n = w.shape[-1]
    return pl.pallas_call(
        _inproj_body,
        out_shape=jax.ShapeDtypeStruct((t, n), BF16),
        grid=(t // tm, n // tn),
        in_specs=[
            pl.BlockSpec((tm, d), lambda i, j: (i, 0)),
            pl.BlockSpec((1, d), lambda i, j: (0, 0)),
            pl.BlockSpec((None, d, tn), lambda i, j: (layer, 0, j)),
        ],
        out_specs=pl.BlockSpec((tm, tn), lambda i, j: (i, j)),
        scratch_shapes=[pltpu.VMEM((tm, d), BF16)],
        compiler_params=_cparams(("parallel", "arbitrary")),
        name="inproj",
    )(x, norm_w.reshape(1, d), w)


def _merge_body(x_ref, ya_ref, yb_ref, yc_ref, ga_ref, gb_ref, gc_ref,
                wa_ref, wb_ref, wc_ref, wo_ref, o_ref):
    @pl.when(pl.program_id(1) == 0)
    def _():
        o_ref[...] = x_ref[...]

    m = jax.nn.sigmoid(ga_ref[...].astype(F32)) * _dot(ya_ref[...], wa_ref[...])
    m += jax.nn.sigmoid(gb_ref[...].astype(F32)) * _dot(yb_ref[...], wb_ref[...])
    m += jax.nn.sigmoid(gc_ref[...].astype(F32)) * _dot(yc_ref[...], wc_ref[...])
    o_ref[...] += _dot(m.astype(BF16), wo_ref[...])


def _merge(x, p, ya, yb, yc, wa, wb, wc, wo, layer, *, tm=512, tn=1024):
    t, d = x.shape
    k = ya.shape[1]
    y_spec = pl.BlockSpec((tm, k), lambda i, j: (i, 0))
    w_spec = pl.BlockSpec((None, k, tn), lambda i, j: (layer, 0, j))

    def gate_spec(col):
        assert col % tn == 0
        return pl.BlockSpec((tm, tn), lambda i, j: (i, col // tn + j))

    return pl.pallas_call(
        _merge_body,
        out_shape=jax.ShapeDtypeStruct((t, d), F32),
        grid=(t // tm, d // tn),
        in_specs=[pl.BlockSpec((tm, d), lambda i, j: (i, 0)), y_spec, y_spec, y_spec,
                  gate_spec(C_GA), gate_spec(C_GB), gate_spec(C_GC),
                  w_spec, w_spec, w_spec,
                  pl.BlockSpec((None, tn, d), lambda i, j: (layer, j, 0))],
        out_specs=pl.BlockSpec((tm, d), lambda i, j: (i, 0)),
        compiler_params=_cparams(("parallel", "arbitrary")),
        name="merge",
    )(x, ya, yb, yc, p, p, p, wa, wb, wc, wo)


def _sgu_body(u_ref, v_ref, nw_ref, nb_ref, ws_ref, bs_ref, o_ref, *, n_chunk):
    v = jax.nn.gelu(v_ref[...].astype(F32))
    mu = jnp.mean(v, axis=-1, keepdims=True)
    vc = v - mu
    var = jnp.mean(vc * vc, axis=-1, keepdims=True)
    vn = (vc * lax.rsqrt(var + EPS) * nw_ref[...] + nb_ref[...]).astype(BF16)
    tril = _iota((SGU_CHUNK, SGU_CHUNK), 0) >= _iota((SGU_CHUNK, SGU_CHUNK), 1)
    for g in range(SGU_GROUPS):
        cols = slice(g * HEAD_DIM, (g + 1) * HEAD_DIM)
        w = jnp.where(tril, ws_ref[g], 0.0).astype(BF16)
        rhs = jnp.concatenate(
            [vn[c * SGU_CHUNK:(c + 1) * SGU_CHUNK, cols] for c in range(n_chunk)], axis=1)
        s = _dot(w, rhs)
        for c in range(n_chunk):
            rows = slice(c * SGU_CHUNK, (c + 1) * SGU_CHUNK)
            u = jax.nn.gelu(u_ref[rows, cols].astype(F32))
            sc = s[:, c * HEAD_DIM:(c + 1) * HEAD_DIM] + bs_ref[g]
            o_ref[rows, cols] = (u * sc).astype(o_ref.dtype)


def _sgu(p, norm_w, norm_b, w_s, b_s, *, n_chunk=4):
    t = p.shape[0]
    ts = n_chunk * SGU_CHUNK
    bias = jnp.broadcast_to(b_s[:, :, None], (SGU_GROUPS, SGU_CHUNK, HEAD_DIM))
    return pl.pallas_call(
        functools.partial(_sgu_body, n_chunk=n_chunk),
        out_shape=jax.ShapeDtypeStruct((t, SGU_WIDTH), BF16),
        grid=(t // ts,),
        in_specs=[
            pl.BlockSpec((ts, SGU_WIDTH), lambda i: (i, C_SU // SGU_WIDTH)),
            pl.BlockSpec((ts, SGU_WIDTH), lambda i: (i, C_SV // SGU_WIDTH)),
            pl.BlockSpec((1, SGU_WIDTH), lambda i: (0, 0)),
            pl.BlockSpec((1, SGU_WIDTH), lambda i: (0, 0)),
            pl.BlockSpec((SGU_GROUPS, SGU_CHUNK, SGU_CHUNK), lambda i: (0, 0, 0)),
            pl.BlockSpec((SGU_GROUPS, SGU_CHUNK, HEAD_DIM), lambda i: (0, 0, 0)),
        ],
        out_specs=pl.BlockSpec((ts, SGU_WIDTH), lambda i: (i, 0)),
        compiler_params=_cparams(("parallel",)),
        name="sgu",
    )(p, p, norm_w.reshape(1, -1), norm_b.reshape(1, -1), w_s, bias)


def _softplus(x):
    return jnp.maximum(x, 0.0) + jnp.log(1.0 + jnp.exp(-jnp.abs(x)))


def _dn_body(q_ref, k_ref, v_ref, z_ref, sm_ref, conv_ref, alog_ref, dtb_ref, onorm_ref,
             o_ref, xbuf, s_ref):
    n = DN_TILE
    shape = (n, n)

    @pl.when(pl.program_id(1) == 0)
    def _():
        xbuf[0:DN_HALO, :] = jnp.zeros((DN_HALO, 3 * DN_WIDTH), F32)
        s_ref[...] = jnp.zeros_like(s_ref)

    xbuf[DN_HALO:, 0:DN_WIDTH] = q_ref[...].astype(F32)
    xbuf[DN_HALO:, DN_WIDTH:2 * DN_WIDTH] = k_ref[...].astype(F32)
    xbuf[DN_HALO:, 2 * DN_WIDTH:] = v_ref[...].astype(F32)

    def conv_silu(c0):
        cols = slice(c0, c0 + HEAD_DIM)
        acc = conv_ref[DN_CONV - 1:DN_CONV, cols] * xbuf[DN_HALO:DN_HALO + n, cols]
        for s in range(1, DN_CONV):
            acc += (conv_ref[DN_CONV - 1 - s:DN_CONV - s, cols]
                    * xbuf[DN_HALO - s:DN_HALO - s + n, cols])
        return acc * jax.nn.sigmoid(acc)

    row = _iota(shape, 0)
    col = _iota(shape, 1)
    same = lax.shift_right_logical(row, 6) == lax.shift_right_logical(col, 6)
    incl = same & (row >= col)
    strict = same & (row > col)
    first_chunk_cols = col < DN_CHUNK
    eye = jnp.where(row == col, 1.0, 0.0).astype(BF16)

    sm = sm_ref[...].astype(F32)
    g_all = -jnp.exp(alog_ref[...]) * _softplus(sm + dtb_ref[...])
    beta_all = jax.nn.sigmoid(sm)
    tri = jnp.where(incl, 1.0, 0.0)
    gc_all = _dot(tri, g_all, HIGHEST)
    last = jnp.where(col == jnp.where(row >= DN_CHUNK, n - 1, DN_CHUNK - 1), 1.0, 0.0)
    gl_all = _dot(last, gc_all, HIGHEST)
    gc_t = gc_all.T

    heads = range(DN_HEADS)
    c = DN_CHUNK
    lows, rhss, attns, q_decs, kd_t0s, kd_t1s, ends0, ends1 = [], [], [], [], [], [], [], []
    for h in heads:
        qa = conv_silu(h * HEAD_DIM)
        ka = conv_silu(DN_WIDTH + h * HEAD_DIM)
        va = conv_silu(2 * DN_WIDTH + h * HEAD_DIM)
        qn = qa * lax.rsqrt(jnp.sum(qa * qa, axis=-1, keepdims=True) + EPS) * HEAD_DIM ** -0.5
        kn = ka * lax.rsqrt(jnp.sum(ka * ka, axis=-1, keepdims=True) + EPS)
        beta = jnp.broadcast_to(beta_all[:, DN_HEADS + h:DN_HEADS + h + 1], shape)
        g_col = jnp.broadcast_to(gc_all[:, h:h + 1], shape)
        g_row = jnp.broadcast_to(gc_t[h:h + 1, :], shape)
        g_end = jnp.broadcast_to(gl_all[:, h:h + 1], shape)
        decay = jnp.where(incl, jnp.exp(jnp.where(incl, g_col - g_row, 0.0)), 0.0)
        e_col = jnp.exp(g_col)
        knb = kn.astype(BF16)
        lows.append(jnp.where(strict, beta * _dot_nt(knb, knb) * decay, 0.0))
        attns.append((_dot_nt(qn.astype(BF16), knb) * decay).astype(BF16))
        rhss.append(jnp.concatenate([va * beta, kn * beta * e_col], axis=1))
        q_decs.append((qn * e_col).astype(BF16))
        kd_t = _dot_nt(eye, (kn * jnp.exp(g_end - g_col)).astype(BF16))
        kd_t0s.append(jnp.where(first_chunk_cols, kd_t, 0.0).astype(BF16))
        kd_t1s.append(jnp.where(first_chunk_cols, 0.0, kd_t).astype(BF16))
        decay_end = jnp.exp(g_end)
        ends0.append(decay_end[0:1, :])
        ends1.append(decay_end[c:c + 1, :])

    ms = [(-low).astype(BF16) for low in lows]
    xs = [rhs + _dot(m, rhs.astype(BF16)) for m, rhs in zip(ms, rhss)]
    for _ in range(5):
        ms = [_dot(m, m).astype(BF16) for m in ms]
        xs = [x + _dot(m, x.astype(BF16)) for m, x in zip(ms, xs)]
    us = [x[:, :HEAD_DIM] for x in xs]
    ws = [x[:, HEAD_DIM:].astype(BF16) for x in xs]

    s0 = [s_ref[h] for h in heads]
    s0b = [s.astype(BF16) for s in s0]
    v0 = [us[h][:c] - _dot(ws[h][:c], s0b[h]) for h in heads]
    o0 = [_dot(q_decs[h][:c], s0b[h]) for h in heads]
    vf0 = [jnp.concatenate([v, jnp.zeros_like(v)], axis=0).astype(BF16) for v in v0]
    s1 = [s0[h] * ends0[h] + _dot(kd_t0s[h], vf0[h]) for h in heads]
    s1b = [s.astype(BF16) for s in s1]
    v1 = [us[h][c:] - _dot(ws[h][c:], s1b[h]) for h in heads]
    o1 = [_dot(q_decs[h][c:], s1b[h]) for h in heads]
    vf = [jnp.concatenate([v0[h], v1[h]], axis=0).astype(BF16) for h in heads]
    for h in heads:
        s_ref[h] = s1[h] * ends1[h] + _dot(kd_t1s[h], vf[h])
    for h in heads:
        hc = slice(h * HEAD_DIM, (h + 1) * HEAD_DIM)
        o = jnp.concatenate([o0[h], o1[h]], axis=0) + _dot(attns[h], vf[h])
        z = z_ref[:, hc].astype(F32)
        o_ref[:, hc] = (_rms(o, onorm_ref[...]) * (z * jax.nn.sigmoid(z))).astype(o_ref.dtype)

    xbuf[0:DN_HALO, :] = xbuf[n:n + DN_HALO, :]


def _deltanet(p3, conv_w, a_log, dt_bias, out_norm):
    b, s, _ = p3.shape
    wide = lambda c: pl.BlockSpec((None, DN_TILE, DN_WIDTH), lambda i, t: (i, t, c // DN_WIDTH))
    lane_pad = lambda a: jnp.pad(a.reshape(1, -1), ((0, 0), (0, LANES - a.shape[-1])))
    full = lambda shp: pl.BlockSpec(shp, lambda i, t: (0,) * len(shp))
    return pl.pallas_call(
        _dn_body,
        out_shape=jax.ShapeDtypeStruct((b, s, DN_WIDTH), BF16),
        grid=(b, s // DN_TILE),
        in_specs=[wide(C_DQ), wide(C_DK), wide(C_DV), wide(C_DZ),
                  pl.BlockSpec((None, DN_TILE, LANES), lambda i, t: (i, t, C_SMALL // LANES)),
                  full((DN_CONV, 3 * DN_WIDTH)), full((1, LANES)), full((1, LANES)),
                  full((1, HEAD_DIM))],
        out_specs=pl.BlockSpec((None, DN_TILE, DN_WIDTH), lambda i, t: (i, t, 0)),
        scratch_shapes=[pltpu.VMEM((DN_HALO + DN_TILE, 3 * DN_WIDTH), F32),
                        pltpu.VMEM((DN_HEADS, HEAD_DIM, HEAD_DIM), F32)],
        compiler_params=_cparams(("parallel", "arbitrary")),
        name="deltanet",
    )(p3, p3, p3, p3, p3, conv_w, lane_pad(a_log), lane_pad(dt_bias), out_norm.reshape(1, -1))


def _pos_features(n, stride, offset):
    pos = _iota((n, LANES), 0) * stride + offset
    lane = _iota((n, LANES), 1)
    feat = jnp.where(lane < POS_SPLIT, lax.shift_right_logical(pos, POS_SHIFT),
                     jnp.where(lane < 2 * POS_SPLIT, pos & ((1 << POS_SHIFT) - 1), 0))
    return feat.astype(F32).astype(BF16)


def _nsa_keys_body(ks_ref, vs_ref, kw_ref, vw_ref, knorm_ref, ks_o, vs_o, kw_o, vw_o):
    rows = ks_o.shape[0]
    feat = _pos_features(rows, 1, pl.program_id(2) * rows)
    ks_o[:, :HEAD_DIM] = _rms(ks_ref[...].astype(F32), knorm_ref[1:2, :]).astype(BF16)
    ks_o[:, HEAD_DIM:] = feat
    vs_o[...] = vs_ref[...]
    kw_o[:, :HEAD_DIM] = _rms(kw_ref[...].astype(F32), knorm_ref[2:3, :]).astype(BF16)
    kw_o[:, HEAD_DIM:] = feat
    vw_o[...] = vw_ref[...]


def _nsa_cmp_body(yk_ref, yv_ref, kpos_ref, kw1_ref, kw2_ref, vpos_ref, vw1_ref, vw2_ref,
                  knorm_ref, kc_o, vc_o):
    def compress(y_ref, pos_ref, w1_ref, w2_ref):
        y = y_ref[...].astype(F32)
        a = _dot((y + pos_ref[0:1, :]).astype(BF16), w1_ref[0])
        b = _dot((y + pos_ref[1:2, :]).astype(BF16), w1_ref[1])
        hid = a + pltpu.roll(b, b.shape[0] - 1, axis=0)
        hid = hid * jax.nn.sigmoid(hid)
        return _dot(hid.astype(BF16), w2_ref[...])

    n_sub = kc_o.shape[0]
    kc_o[:, :HEAD_DIM] = _rms(compress(yk_ref, kpos_ref, kw1_ref, kw2_ref),
                              knorm_ref[0:1, :]).astype(BF16)
    kc_o[:, HEAD_DIM:] = _pos_features(n_sub, CMP_STRIDE, CMP_BLOCK - 1)
    vc_o[...] = compress(yv_ref, vpos_ref, vw1_ref, vw2_ref).astype(BF16)


def _nsa_prep(p3, k_norm, cmpk_pos, cmpk_w1, cmpk_w2, cmpv_pos, cmpv_w1, cmpv_w2, *, rows=2048):
    b, s, _ = p3.shape
    g = NSA_GROUPS
    n_sub = s // CMP_STRIDE
    half = CMP_STRIDE * HEAD_DIM
    kv = g * HEAD_DIM
    k_aug = 2 * HEAD_DIM

    def stride_chunks(c0):
        y = p3[:, :, c0:c0 + kv].reshape(b, n_sub, CMP_STRIDE, g, HEAD_DIM)
        return jnp.transpose(y, (0, 3, 1, 2, 4)).reshape(b, g, n_sub, half)

    full2 = lambda shp: pl.BlockSpec(shp, lambda i, j: (0,) * len(shp))
    cmp_io = lambda w: pl.BlockSpec((None, None, n_sub, w), lambda i, j: (i, j, 0, 0))
    cmp_shape = lambda w: jax.ShapeDtypeStruct((b, g, n_sub, w), BF16)
    kc, vc = pl.pallas_call(
        _nsa_cmp_body,
        out_shape=(cmp_shape(k_aug), cmp_shape(HEAD_DIM)),
        grid=(b, g),
        in_specs=[cmp_io(half), cmp_io(half),
                  full2((2, half)), full2((2, half, CMP_HIDDEN)), full2((CMP_HIDDEN, HEAD_DIM)),
                  full2((2, half)), full2((2, half, CMP_HIDDEN)), full2((CMP_HIDDEN, HEAD_DIM)),
                  full2((3, HEAD_DIM))],
        out_specs=(cmp_io(k_aug), cmp_io(HEAD_DIM)),
        compiler_params=_cparams(("parallel", "parallel")),
        name="nsa_cmp",
    )(stride_chunks(C_NKV), stride_chunks(C_NKV + kv),
      cmpk_pos.reshape(2, half), cmpk_w1.reshape(2, half, CMP_HIDDEN).astype(BF16),
      cmpk_w2.astype(BF16),
      cmpv_pos.reshape(2, half), cmpv_w1.reshape(2, half, CMP_HIDDEN).astype(BF16),
      cmpv_w2.astype(BF16), k_norm)

    rows = min(rows, s)
    col = lambda c0: pl.BlockSpec((None, rows, HEAD_DIM),
                                  lambda i, j, t: (i, t, c0 // HEAD_DIM + j))
    key_io = lambda w: pl.BlockSpec((None, None, rows, w), lambda i, j, t: (i, j, t, 0))
    key_shape = lambda w: jax.ShapeDtypeStruct((b, g, s, w), BF16)
    ks, vs, kw, vw = pl.pallas_call(
        _nsa_keys_body,
        out_shape=(key_shape(k_aug), key_shape(HEAD_DIM), key_shape(k_aug), key_shape(HEAD_DIM)),
        grid=(b, g, s // rows),
        in_specs=[col(C_NKV + 2 * kv), col(C_NKV + 3 * kv), col(C_NKV + 4 * kv), col(C_NKV + 5 * kv),
                  pl.BlockSpec((3, HEAD_DIM), lambda i, j, t: (0, 0))],
        out_specs=(key_io(k_aug), key_io(HEAD_DIM), key_io(k_aug), key_io(HEAD_DIM)),
        compiler_params=_cparams(("parallel", "parallel", "parallel")),
        name="nsa_keys",
    )(p3, p3, p3, p3, k_norm)
    return kc, vc, ks, vs, kw, vw


def _split3(x):
    hi = x.astype(BF16).astype(F32)
    mid = (x - hi).astype(BF16).astype(F32)
    return hi, mid, x - hi - mid


def _nsa_body(q_ref, sm_ref, kc_ref, vc_ref, ks_ref, vs_ref, kw_ref, vw_ref,
              qn_ref, ovt_ref, e_ref, tg_ref, o_ref,
              m_ref, l_ref, acc_ref, part_ref, flag_ref, *, top_n):
    grp = pl.program_id(1)
    qb = pl.program_id(2)
    t0 = qb * Q_TILE
    nq = Q_TILE
    nr = NSA_HPG * nq
    ncp = kc_ref.shape[0]
    hrows = [slice(h * nq, (h + 1) * nq) for h in range(NSA_HPG)]
    hcols = [slice(h * HEAD_DIM, (h + 1) * HEAD_DIM) for h in range(NSA_HPG)]

    head = (lax.shift_right_logical(_iota((nr, 1), 0), Q_SHIFT) + grp * NSA_HPG + 1).astype(F32)
    slope = jnp.exp(head * (-ALIBI_MAX / NSA_HEADS * np.log(2.0))) * LOG2E
    lane = _iota((nr, LANES), 1)
    slope_feat = jnp.zeros((nr, LANES), F32)
    for i, term in enumerate(_split3(slope)):
        slope_feat = jnp.where(lane == i, term * float(1 << POS_SHIFT), slope_feat)
        slope_feat = jnp.where(lane == POS_SPLIT + i, term, slope_feat)
    qh = jnp.concatenate(
        [_rms(q_ref[:, hcols[h]].astype(F32), qn_ref[...]) * (HEAD_DIM ** -0.5 * LOG2E)
         for h in range(NSA_HPG)], axis=0)
    q = jnp.concatenate([qh, slope_feat], axis=1).astype(BF16)

    def softmax_rows(s, mask, row_ok=None):
        s = jnp.where(mask, s, NEG)
        e = jnp.exp2(s - jnp.max(s, axis=-1, keepdims=True))
        inv = 1.0 / jnp.sum(e, axis=-1, keepdims=True)
        return e * (inv if row_ok is None else jnp.where(row_ok, inv, 0.0))

    gates = jax.nn.sigmoid(sm_ref[...].astype(F32))

    def gate(branch, h):
        at = lambda gv: 2 * DN_HEADS + branch * NSA_HEADS + gv * NSA_HPG + h
        cols = [gates[:, at(gv):at(gv) + 1] for gv in range(NSA_GROUPS)]
        return jnp.where(grp == 0, cols[0], cols[1])

    s_c = _dot_nt(q, kc_ref[...])
    mask_c = (t0 + _iota((nq, ncp), 0)) >= (_iota((nq, ncp), 1) * CMP_STRIDE + (CMP_BLOCK - 1))
    sees_block = (t0 + _iota((nq, 1), 0)) >= CMP_BLOCK - 1
    p_c = [softmax_rows(s_c[hrows[h]], mask_c, sees_block) for h in range(NSA_HPG)]
    o_c = _dot(jnp.concatenate([p.astype(BF16) for p in p_c], axis=0), vc_ref[...])

    p_grp = p_c[0]
    for h in range(1, NSA_HPG):
        p_grp = p_grp + p_c[h]
    imp = _dot_nt(ovt_ref[...], p_grp, HIGHEST)
    blk = _iota((LANES, nq), 0)
    cur = lax.shift_right_logical(t0 + _iota((LANES, nq), 1), 6)
    forced = (blk == 0) | (blk == cur) | (blk == cur - 1)
    score = jnp.where(forced, -3e38, jnp.where(blk <= cur, imp, -1e6))
    sel_t = jnp.where(forced, 1.0, 0.0)
    blk_f = blk.astype(F32)

    wk = WINDOW + nq
    w0 = pl.multiple_of(jnp.maximum(t0 - WINDOW, 0), nq)
    s_w = _dot_nt(q, kw_ref[pl.ds(w0, wk), :])
    dist_w = (t0 + _iota((nq, wk), 0)) - (w0 + _iota((nq, wk), 1))
    mask_w = (dist_w >= 0) & (dist_w < WINDOW)
    p_w = []
    rounds = top_n - 3
    for r in range(rounds):
        best = jnp.max(score, axis=0, keepdims=True)
        first = jnp.min(jnp.where(score == best, blk_f, float(LANES)), axis=0, keepdims=True)
        pick = blk_f == first
        sel_t = jnp.where(pick, 1.0, sel_t)
        score = jnp.where(pick, -3e38, score)
        if (r + 1) * NSA_HPG // rounds > len(p_w):
            p_w.append(softmax_rows(s_w[hrows[len(p_w)]], mask_w).astype(BF16))
    eye = jnp.where(_iota((nq, nq), 0) == _iota((nq, nq), 1), 1.0, 0.0).astype(BF16)
    sel = _dot_nt(eye, sel_t.astype(BF16)).astype(BF16)
    o_w = _dot(jnp.concatenate(p_w, axis=0), vw_ref[pl.ds(w0, wk), :])
    for h in range(NSA_HPG):
        part_ref[:, hcols[h]] = gate(0, h) * o_c[hrows[h]] + gate(2, h) * o_w[hrows[h]]

    picked = jnp.broadcast_to(jnp.sum(sel.astype(F32), axis=0, keepdims=True), (8, LANES))
    tile_count = _dot(picked.astype(BF16), tg_ref[...])
    for t in range(flag_ref.shape[0]):
        flag_ref[t] = (tile_count[0, t] > 0.0).astype(jnp.int32)

    m_ref[...] = jnp.full(m_ref.shape, NEG, F32)
    l_ref[...] = jnp.zeros(l_ref.shape, F32)
    acc_ref[...] = jnp.zeros(acc_ref.shape, F32)
    kt_diag = lax.shift_right_logical(t0, 9)

    def sel_tile(kt, causal):
        k0 = pl.multiple_of(kt * SEL_KT, SEL_KT)
        s = _dot_nt(q, ks_ref[pl.ds(k0, SEL_KT), :])
        mask = _dot(sel, e_ref[kt]) > 0.5
        if causal:
            mask = mask & ((t0 + _iota((nq, SEL_KT), 0)) >= (k0 + _iota((nq, SEL_KT), 1)))
        ps, alphas = [], []
        for h in range(NSA_HPG):
            sh = jnp.where(mask, s[hrows[h]], NEG)
            m_prev = m_ref[hrows[h], :]
            m_new = jnp.maximum(m_prev, jnp.max(sh, axis=-1, keepdims=True))
            alpha = jnp.exp2(m_prev - m_new)
            p = jnp.exp2(sh - m_new[:, 0:1])
            l_ref[hrows[h], :] = alpha * l_ref[hrows[h], :] + jnp.sum(p, axis=-1, keepdims=True)
            m_ref[hrows[h], :] = m_new
            ps.append(p.astype(BF16))
            alphas.append(alpha)
        pv = _dot(jnp.concatenate(ps, axis=0), vs_ref[pl.ds(k0, SEL_KT), :])
        acc_ref[...] = jnp.concatenate(alphas, axis=0) * acc_ref[...] + pv

    sel_tile(kt_diag, True)

    def earlier_tile(i, carry):
        kt = kt_diag - 1 - i

        @pl.when(flag_ref[kt] > 0)
        def _():
            sel_tile(kt, False)

        return carry

    lax.fori_loop(0, kt_diag, earlier_tile, 0)
    o_s = acc_ref[...] / l_ref[...]
    gates = jax.nn.sigmoid(sm_ref[...].astype(F32))
    for h in range(NSA_HPG):
        o = part_ref[:, hcols[h]] + gate(1, h) * o_s[hrows[h]]
        o_ref[:, hcols[h]] = o.astype(o_ref.dtype)


def _nsa(p3, kc, vc, ks, vs, kw, vw, q_norm):
    b, s, _ = p3.shape
    g = NSA_GROUPS
    ncp = s // CMP_STRIDE
    n_sel = s // SEL_BLOCK
    assert n_sel <= LANES and s % SEL_KT == 0 and s >= WINDOW + Q_TILE and NSA_GROUPS == 2
    assert SEL_KT % Q_TILE == 0 and WINDOW % Q_TILE == 0
    assert min(SEL_TOPN, n_sel) - 3 >= NSA_HPG
    gw = NSA_HPG * HEAD_DIM
    ci = np.arange(ncp)[:, None]
    sj = np.arange(LANES)[None, :]
    per = SEL_BLOCK // CMP_STRIDE
    overlap = ((ci // per == sj).astype(np.float32) + ((ci + 1) // per == sj).astype(np.float32))
    overlap[ncp - 1:] = 0.0
    key_blk = (np.arange(s) // SEL_BLOCK).reshape(s // SEL_KT, 1, SEL_KT)
    expand = (key_blk == np.arange(LANES)[None, :, None]).astype(np.float32)

    n_tiles = s // SEL_KT
    tile_of = (np.arange(LANES)[:, None] // (SEL_KT // SEL_BLOCK)
               == np.arange(LANES)[None, :]).astype(np.float32)

    kv = lambda rows, w: pl.BlockSpec((None, None, rows, w), lambda i, j, t: (i, j, 0, 0))
    full = lambda shp: pl.BlockSpec(shp, lambda i, j, t: (0,) * len(shp))
    return pl.pallas_call(
        functools.partial(_nsa_body, top_n=min(SEL_TOPN, n_sel)),
        out_shape=jax.ShapeDtypeStruct((b, s, NSA_WIDTH), BF16),
        grid=(b, g, s // Q_TILE),
        in_specs=[pl.BlockSpec((None, Q_TILE, gw), lambda i, j, t: (i, t, C_NQ // gw + j)),
                  pl.BlockSpec((None, Q_TILE, LANES), lambda i, j, t: (i, t, C_SMALL // LANES)),
                  kv(ncp, 2 * HEAD_DIM), kv(ncp, HEAD_DIM), kv(s, 2 * HEAD_DIM), kv(s, HEAD_DIM),
                  kv(s, 2 * HEAD_DIM), kv(s, HEAD_DIM),
                  full((1, HEAD_DIM)), full((LANES, ncp)), full((n_tiles, LANES, SEL_KT)),
                  full((LANES, LANES))],
        out_specs=pl.BlockSpec((None, Q_TILE, gw), lambda i, j, t: (i, t, j)),
        scratch_shapes=[pltpu.VMEM((NSA_HPG * Q_TILE, HEAD_DIM), F32)] * 3
        + [pltpu.VMEM((Q_TILE, gw), F32), pltpu.SMEM((n_tiles,), jnp.int32)],
        compiler_params=_cparams(("parallel", "parallel", "arbitrary")),
        name="nsa",
    )(p3, p3, kc, vc, ks, vs, kw, vw, q_norm.reshape(1, -1),
      jnp.asarray(overlap.T), jnp.asarray(expand, dtype=BF16), jnp.asarray(tile_of, dtype=BF16))


def kernel(x, ffn1_norm, ffn1_gate, ffn1_up, ffn1_down, mix_norm, w_in, dn_conv, dn_a_log,
           dn_dt_bias, dn_out_norm, nsa_q_norm, nsa_k_norm, cmpk_pos, cmpk_w1, cmpk_w2,
           cmpv_pos, cmpv_w1, cmpv_w2, sgu_norm_w, sgu_norm_b, sgu_w, sgu_b, w_branch_a,
           w_branch_b, w_branch_c, w_out, ffn2_norm, ffn2_gate, ffn2_up, ffn2_down):
    batch, seq, d = x.shape
    t = batch * seq
    ffn1 = _ffn_weights(ffn1_gate, ffn1_up, ffn1_down)
    ffn2 = _ffn_weights(ffn2_gate, ffn2_up, ffn2_down)
    w_in = _reorder_w_in(w_in)
    mix = tuple(w.astype(BF16) for w in (w_branch_a, w_branch_b, w_branch_c, w_out))

    h = x.reshape(t, d)
    for l in range(ffn1_norm.shape[0]):
        h = _ffn(h, ffn1_norm[l], *ffn1, l)
        p = _inproj(h, mix_norm[l], w_in, l)
        p3 = p.reshape(batch, seq, P_COLS)
        y_a = _deltanet(p3, dn_conv[l], dn_a_log[l], dn_dt_bias[l], dn_out_norm[l])
        kv = _nsa_prep(p3, nsa_k_norm[l], cmpk_pos[l], cmpk_w1[l], cmpk_w2[l],
                       cmpv_pos[l], cmpv_w1[l], cmpv_w2[l])
        y_b = _nsa(p3, *kv, nsa_q_norm[l])
        y_c = _sgu(p, sgu_norm_w[l], sgu_norm_b[l], sgu_w[l], sgu_b[l])
        h = _merge(h, p, y_a.reshape(t, -1), y_b.reshape(t, -1), y_c, *mix, l)
        h = _ffn(h, ffn2_norm[l], *ffn2, l)
    return h.reshape(batch, seq, d)
```

```python
import functools

import jax
import jax.numpy as jnp
import numpy as np
from jax import lax
from jax.experimental import pallas as pl
from jax.experimental.pallas import tpu as pltpu

F32 = jnp.float32
BF16 = jnp.bfloat16
HIGHEST = lax.Precision.HIGHEST

LANES = 128
V7X_VMEM_LIMIT = 56 * 1024 * 1024

HEAD_DIM = 128
EPS = 1e-6
NEG = -1e30

DN_HEADS = 8
DN_WIDTH = DN_HEADS * HEAD_DIM
DN_CONV = 4
DN_CHUNK = 64
DN_TILE = 2 * DN_CHUNK
DN_HALO = 8

NSA_HEADS = 8
NSA_GROUPS = 2
NSA_HPG = NSA_HEADS // NSA_GROUPS
NSA_WIDTH = NSA_HEADS * HEAD_DIM
CMP_STRIDE = 16
CMP_BLOCK = 2 * CMP_STRIDE
CMP_HIDDEN = 256
SEL_BLOCK = 64
SEL_TOPN = 16
WINDOW = 512
Q_SHIFT = 8
Q_TILE = 1 << Q_SHIFT
ALIBI_MAX = 8.0
SEL_KT = 512
POS_SHIFT = 6
POS_SPLIT = 3
LOG2E = 1.4426950408889634

SGU_GROUPS = 8
SGU_WIDTH = SGU_GROUPS * HEAD_DIM
SGU_CHUNK = 128


def _cparams(sem):
    return pltpu.CompilerParams(dimension_semantics=sem, vmem_limit_bytes=V7X_VMEM_LIMIT)


def _rms(x, w):
    return x * lax.rsqrt(jnp.mean(x * x, axis=-1, keepdims=True) + EPS) * w


def _iota(shape, dim):
    return lax.broadcasted_iota(jnp.int32, shape, dim)


def _dot(a, b, precision=None):
    return jnp.dot(a, b, preferred_element_type=F32, precision=precision)


def _dot_nt(a, b, precision=None):
    return lax.dot_general(a, b, (((1,), (1,)), ((), ())), preferred_element_type=F32,
                           precision=precision)


def _ffn_body(x_ref, nw_ref, wg_ref, wu_ref, wd_ref, o_ref, h_ref):
    @pl.when(pl.program_id(1) == 0)
    def _():
        x = x_ref[...]
        h_ref[...] = _rms(x, nw_ref[...]).astype(BF16)
        o_ref[...] = x

    h = h_ref[...]
    g = _dot(h, wg_ref[...])
    u = _dot(h, wu_ref[...])
    a = (0.5 * g * jax.nn.sigmoid(g) * u).astype(BF16)
    o_ref[...] += _dot(a, wd_ref[...])


FF_TILE = 512


def _ffn_weights(w_gate, w_up, w_down):
    ff = w_gate.shape[-1]
    extra = FF_TILE * pl.cdiv(ff, FF_TILE) - ff
    cols = lambda w: jnp.pad(w, ((0, 0), (0, 0), (0, extra))).astype(BF16)
    return cols(w_gate), cols(w_up), jnp.pad(w_down, ((0, 0), (0, extra), (0, 0))).astype(BF16)


def _ffn(x, norm_w, wg, wu, wd, layer, *, tm=1024, tf=FF_TILE):
    t, d = x.shape
    ffp = wg.shape[-1]
    return pl.pallas_call(
        _ffn_body,
        out_shape=jax.ShapeDtypeStruct((t, d), F32),
        grid=(t // tm, ffp // tf),
        in_specs=[
            pl.BlockSpec((tm, d), lambda i, j: (i, 0)),
            pl.BlockSpec((1, d), lambda i, j: (0, 0)),
            pl.BlockSpec((None, d, tf), lambda i, j: (layer, 0, j)),
            pl.BlockSpec((None, d, tf), lambda i, j: (layer, 0, j)),
            pl.BlockSpec((None, tf, d), lambda i, j: (layer, j, 0)),
        ],
        out_specs=pl.BlockSpec((tm, d), lambda i, j: (i, 0)),
        scratch_shapes=[pltpu.VMEM((tm, d), BF16)],
        compiler_params=_cparams(("parallel", "arbitrary")),
        name="ffn",
    )(x, norm_w.reshape(1, d), wg, wu, wd)


C_DQ, C_DK, C_DV, C_DZ = 0, 1024, 2048, 3072
C_NQ = 4096
C_SU, C_SV = 5120, 6144
C_GA, C_GB, C_GC = 7168, 9216, 11264
C_NKV = 13312
C_SMALL = 14848
P_COLS = 15360


def _reorder_w_in(w_in):
    w_in = w_in.astype(BF16)
    pad = jnp.zeros(w_in.shape[:-1] + (P_COLS - w_in.shape[-1],), BF16)
    return jnp.concatenate([
        w_in[..., 0:4096],
        w_in[..., 4112:5136],
        w_in[..., 6696:8744],
        w_in[..., 8744:14888],
        w_in[..., 5136:6672],
        w_in[..., 4096:4112],
        w_in[..., 6672:6696],
        pad], axis=-1)


def _inproj_body(x_ref, nw_ref, w_ref, o_ref, h_ref):
    @pl.when(pl.program_id(1) == 0)
    def _():
        h_ref[...] = _rms(x_ref[...], nw_ref[...]).astype(BF16)

    o_ref[...] = _dot(h_ref[...], w_ref[...]).astype(o_ref.dtype)


def _inproj(x, norm_w, w, layer, *, tm=1024, tn=1536):
    t, d = x.shape
    n = w.shape[-1]
    return pl.pallas_call(
        _inproj_body,
        out_shape=jax.ShapeDtypeStruct((t, n), BF16),
        grid=(t // tm, n // tn),
        in_specs=[
            pl.BlockSpec((tm, d), lambda i, j: (i, 0)),
            pl.BlockSpec((1, d), lambda i, j: (0, 0)),
            pl.BlockSpec((None, d, tn), lambda i, j: (layer, 0, j)),
        ],
        out_specs=pl.BlockSpec((tm, tn), lambda i, j: (i, j)),
        scratch_shapes=[pltpu.VMEM((tm, d), BF16)],
        compiler_params=_cparams(("parallel", "arbitrary")),
        name="inproj",
    )(x, norm_w.reshape(1, d), w)


def _merge_body(x_ref, ya_ref, yb_ref, yc_ref, ga_ref, gb_ref, gc_ref,
                wa_ref, wb_ref, wc_ref, wo_ref, o_ref):
    @pl.when(pl.program_id(1) == 0)
    def _():
        o_ref[...] = x_ref[...]

    m = jax.nn.sigmoid(ga_ref[...].astype(F32)) * _dot(ya_ref[...], wa_ref[...])
    m += jax.nn.sigmoid(gb_ref[...].astype(F32)) * _dot(yb_ref[...], wb_ref[...])
    m += jax.nn.sigmoid(gc_ref[...].astype(F32)) * _dot(yc_ref[...], wc_ref[...])
    o_ref[...] += _dot(m.astype(BF16), wo_ref[...])


def _merge(x, p, ya, yb, yc, wa, wb, wc, wo, layer, *, tm=512, tn=1024):
    t, d = x.shape
    k = ya.shape[1]
    y_spec = pl.BlockSpec((tm, k), lambda i, j: (i, 0))
    w_spec = pl.BlockSpec((None, k, tn), lambda i, j: (layer, 0, j))

    def gate_spec(col):
        assert col % tn == 0
        return pl.BlockSpec((tm, tn), lambda i, j: (i, col // tn + j))

    return pl.pallas_call(
        _merge_body,
        out_shape=jax.ShapeDtypeStruct((t, d), F32),
        grid=(t // tm, d // tn),
        in_specs=[pl.BlockSpec((tm, d), lambda i, j: (i, 0)), y_spec, y_spec, y_spec,
                  gate_spec(C_GA), gate_spec(C_GB), gate_spec(C_GC),
                  w_spec, w_spec, w_spec,
                  pl.BlockSpec((None, tn, d), lambda i, j: (layer, j, 0))],
        out_specs=pl.BlockSpec((tm, d), lambda i, j: (i, 0)),
        compiler_params=_cparams(("parallel", "arbitrary")),
        name="merge",
    )(x, ya, yb, yc, p, p, p, wa, wb, wc, wo)


def _sgu_body(u_ref, v_ref, nw_ref, nb_ref, ws_ref, bs_ref, o_ref, *, n_chunk):
    v = jax.nn.gelu(v_ref[...].astype(F32))
    mu = jnp.mean(v, axis=-1, keepdims=True)
    vc = v - mu
    var = jnp.mean(vc * vc, axis=-1, keepdims=True)
    vn = (vc * lax.rsqrt(var + EPS) * nw_ref[...] + nb_ref[...]).astype(BF16)
    tril = _iota((SGU_CHUNK, SGU_CHUNK), 0) >= _iota((SGU_CHUNK, SGU_CHUNK), 1)
    for g in range(SGU_GROUPS):
        cols = slice(g * HEAD_DIM, (g + 1) * HEAD_DIM)
        w = jnp.where(tril, ws_ref[g], 0.0).astype(BF16)
        rhs = jnp.concatenate(
            [vn[c * SGU_CHUNK:(c + 1) * SGU_CHUNK, cols] for c in range(n_chunk)], axis=1)
        s = _dot(w, rhs)
        for c in range(n_chunk):
            rows = slice(c * SGU_CHUNK, (c + 1) * SGU_CHUNK)
            u = jax.nn.gelu(u_ref[rows, cols].astype(F32))
            sc = s[:, c * HEAD_DIM:(c + 1) * HEAD_DIM] + bs_ref[g]
            o_ref[rows, cols] = (u * sc).astype(o_ref.dtype)


def _sgu(p, norm_w, norm_b, w_s, b_s, *, n_chunk=4):
    t = p.shape[0]
    ts = n_chunk * SGU_CHUNK
    bias = jnp.broadcast_to(b_s[:, :, None], (SGU_GROUPS, SGU_CHUNK, HEAD_DIM))
    return pl.pallas_call(
        functools.partial(_sgu_body, n_chunk=n_chunk),
        out_shape=jax.ShapeDtypeStruct((t, SGU_WIDTH), BF16),
        grid=(t // ts,),
        in_specs=[
            pl.BlockSpec((ts, SGU_WIDTH), lambda i: (i, C_SU // SGU_WIDTH)),
            pl.BlockSpec((ts, SGU_WIDTH), lambda i: (i, C_SV // SGU_WIDTH)),
            pl.BlockSpec((1, SGU_WIDTH), lambda i: (0, 0)),
            pl.BlockSpec((1, SGU_WIDTH), lambda i: (0, 0)),
            pl.BlockSpec((SGU_GROUPS, SGU_CHUNK, SGU_CHUNK), lambda i: (0, 0, 0)),
            pl.BlockSpec((SGU_GROUPS, SGU_CHUNK, HEAD_DIM), lambda i: (0, 0, 0)),
        ],
        out_specs=pl.BlockSpec((ts, SGU_WIDTH), lambda i: (i, 0)),
        compiler_params=_cparams(("parallel",)),
        name="sgu",
    )(p, p, norm_w.reshape(1, -1), norm_b.reshape(1, -1), w_s, bias)


def _softplus(x):
    return jnp.maximum(x, 0.0) + jnp.log(1.0 + jnp.exp(-jnp.abs(x)))


def _dn_body(q_ref, k_ref, v_ref, z_ref, sm_ref, conv_ref, alog_ref, dtb_ref, onorm_ref,
             o_ref, xbuf, s_ref):
    n = DN_TILE
    shape = (n, n)

    @pl.when(pl.program_id(1) == 0)
    def _():
        xbuf[0:DN_HALO, :] = jnp.zeros((DN_HALO, 3 * DN_WIDTH), F32)
        s_ref[...] = jnp.zeros_like(s_ref)

    xbuf[DN_HALO:, 0:DN_WIDTH] = q_ref[...].astype(F32)
    xbuf[DN_HALO:, DN_WIDTH:2 * DN_WIDTH] = k_ref[...].astype(F32)
    xbuf[DN_HALO:, 2 * DN_WIDTH:] = v_ref[...].astype(F32)

    def conv_silu(c0):
        cols = slice(c0, c0 + HEAD_DIM)
        acc = conv_ref[DN_CONV - 1:DN_CONV, cols] * xbuf[DN_HALO:DN_HALO + n, cols]
        for s in range(1, DN_CONV):
            acc += (conv_ref[DN_CONV - 1 - s:DN_CONV - s, cols]
                    * xbuf[DN_HALO - s:DN_HALO - s + n, cols])
        return acc * jax.nn.sigmoid(acc)

    row = _iota(shape, 0)
    col = _iota(shape, 1)
    same = lax.shift_right_logical(row, 6) == lax.shift_right_logical(col, 6)
    incl = same & (row >= col)
    strict = same & (row > col)
    first_chunk_cols = col < DN_CHUNK
    eye = jnp.where(row == col, 1.0, 0.0).astype(BF16)

    sm = sm_ref[...].astype(F32)
    g_all = -jnp.exp(alog_ref[...]) * _softplus(sm + dtb_ref[...])
    beta_all = jax.nn.sigmoid(sm)
    tri = jnp.where(incl, 1.0, 0.0)
    gc_all = _dot(tri, g_all, HIGHEST)
    last = jnp.where(col == jnp.where(row >= DN_CHUNK, n - 1, DN_CHUNK - 1), 1.0, 0.0)
    gl_all = _dot(last, gc_all, HIGHEST)
    gc_t = gc_all.T

    heads = range(DN_HEADS)
    c = DN_CHUNK
    lows, rhss, attns, q_decs, kd_t0s, kd_t1s, ends0, ends1 = [], [], [], [], [], [], [], []
    for h in heads:
        qa = conv_silu(h * HEAD_DIM)
        ka = conv_silu(DN_WIDTH + h * HEAD_DIM)
        va = conv_silu(2 * DN_WIDTH + h * HEAD_DIM)
        qn = qa * lax.rsqrt(jnp.sum(qa * qa, axis=-1, keepdims=True) + EPS) * HEAD_DIM ** -0.5
        kn = ka * lax.rsqrt(jnp.sum(ka * ka, axis=-1, keepdims=True) + EPS)
        beta = jnp.broadcast_to(beta_all[:, DN_HEADS + h:DN_HEADS + h + 1], shape)
        g_col = jnp.broadcast_to(gc_all[:, h:h + 1], shape)
        g_row = jnp.broadcast_to(gc_t[h:h + 1, :], shape)
        g_end = jnp.broadcast_to(gl_all[:, h:h + 1], shape)
        decay = jnp.where(incl, jnp.exp(jnp.where(incl, g_col - g_row, 0.0)), 0.0)
        e_col = jnp.exp(g_col)
        knb = kn.astype(BF16)
        lows.append(jnp.where(strict, beta * _dot_nt(knb, knb) * decay, 0.0))
        attns.append((_dot_nt(qn.astype(BF16), knb) * decay).astype(BF16))
        rhss.append(jnp.concatenate([va * beta, kn * beta * e_col], axis=1))
        q_decs.append((qn * e_col).astype(BF16))
        kd_t = _dot_nt(eye, (kn * jnp.exp(g_end - g_col)).astype(BF16))
        kd_t0s.append(jnp.where(first_chunk_cols, kd_t, 0.0).astype(BF16))
        kd_t1s.append(jnp.where(first_chunk_cols, 0.0, kd_t).astype(BF16))
        decay_end = jnp.exp(g_end)
        ends0.append(decay_end[0:1, :])
        ends1.append(decay_end[c:c + 1, :])

    ms = [(-low).astype(BF16) for low in lows]
    xs = [rhs + _dot(m, rhs.astype(BF16)) for m, rhs in zip(ms, rhss)]
    for _ in range(5):
        ms = [_dot(m, m).astype(BF16) for m in ms]
        xs = [x + _dot(m, x.astype(BF16)) for m, x in zip(ms, xs)]
    us = [x[:, :HEAD_DIM] for x in xs]
    ws = [x[:, HEAD_DIM:].astype(BF16) for x in xs]

    s0 = [s_ref[h] for h in heads]
    s0b = [s.astype(BF16) for s in s0]
    v0 = [us[h][:c] - _dot(ws[h][:c], s0b[h]) for h in heads]
    o0 = [_dot(q_decs[h][:c], s0b[h]) for h in heads]
    vf0 = [jnp.concatenate([v, jnp.zeros_like(v)], axis=0).astype(BF16) for v in v0]
    s1 = [s0[h] * ends0[h] + _dot(kd_t0s[h], vf0[h]) for h in heads]
    s1b = [s.astype(BF16) for s in s1]
    v1 = [us[h][c:] - _dot(ws[h][c:], s1b[h]) for h in heads]
    o1 = [_dot(q_decs[h][c:], s1b[h]) for h in heads]
    vf = [jnp.concatenate([v0[h], v1[h]], axis=0).astype(BF16) for h in heads]
    for h in heads:
        s_ref[h] = s1[h] * ends1[h] + _dot(kd_t1s[h], vf[h])
    for h in heads:
        hc = slice(h * HEAD_DIM, (h + 1) * HEAD_DIM)
        o = jnp.concatenate([o0[h], o1[h]], axis=0) + _dot(attns[h], vf[h])
        z = z_ref[:, hc].astype(F32)
        o_ref[:, hc] = (_rms(o, onorm_ref[...]) * (z * jax.nn.sigmoid(z))).astype(o_ref.dtype)

    xbuf[0:DN_HALO, :] = xbuf[n:n + DN_HALO, :]


def _deltanet(p3, conv_w, a_log, dt_bias, out_norm):
    b, s, _ = p3.shape
    wide = lambda c: pl.BlockSpec((None, DN_TILE, DN_WIDTH), lambda i, t: (i, t, c // DN_WIDTH))
    lane_pad = lambda a: jnp.pad(a.reshape(1, -1), ((0, 0), (0, LANES - a.shape[-1])))
    full = lambda shp: pl.BlockSpec(shp, lambda i, t: (0,) * len(shp))
    return pl.pallas_call(
        _dn_body,
        out_shape=jax.ShapeDtypeStruct((b, s, DN_WIDTH), BF16),
        grid=(b, s // DN_TILE),
        in_specs=[wide(C_DQ), wide(C_DK), wide(C_DV), wide(C_DZ),
                  pl.BlockSpec((None, DN_TILE, LANES), lambda i, t: (i, t, C_SMALL // LANES)),
                  full((DN_CONV, 3 * DN_WIDTH)), full((1, LANES)), full((1, LANES)),
                  full((1, HEAD_DIM))],
        out_specs=pl.BlockSpec((None, DN_TILE, DN_WIDTH), lambda i, t: (i, t, 0)),
        scratch_shapes=[pltpu.VMEM((DN_HALO + DN_TILE, 3 * DN_WIDTH), F32),
                        pltpu.VMEM((DN_HEADS, HEAD_DIM, HEAD_DIM), F32)],
        compiler_params=_cparams(("parallel", "arbitrary")),
        name="deltanet",
    )(p3, p3, p3, p3, p3, conv_w, lane_pad(a_log), lane_pad(dt_bias), out_norm.reshape(1, -1))


def _pos_features(n, stride, offset):
    pos = _iota((n, LANES), 0) * stride + offset
    lane = _iota((n, LANES), 1)
    feat = jnp.where(lane < POS_SPLIT, lax.shift_right_logical(pos, POS_SHIFT),
                     jnp.where(lane < 2 * POS_SPLIT, pos & ((1 << POS_SHIFT) - 1), 0))
    return feat.astype(F32).astype(BF16)


def _nsa_keys_body(ks_ref, vs_ref, kw_ref, vw_ref, knorm_ref, ks_o, vs_o, kw_o, vw_o):
    rows = ks_o.shape[0]
    feat = _pos_features(rows, 1, pl.program_id(2) * rows)
    ks_o[:, :HEAD_DIM] = _rms(ks_ref[...].astype(F32), knorm_ref[1:2, :]).astype(BF16)
    ks_o[:, HEAD_DIM:] = feat
    vs_o[...] = vs_ref[...]
    kw_o[:, :HEAD_DIM] = _rms(kw_ref[...].astype(F32), knorm_ref[2:3, :]).astype(BF16)
    kw_o[:, HEAD_DIM:] = feat
    vw_o[...] = vw_ref[...]


def _nsa_cmp_body(xk_ref, xv_ref, kpos_ref, kw1_ref, kw2_ref, vpos_ref, vw1_ref, vw2_ref,
                  knorm_ref, kc_o, vc_o, xf_ref):
    n_sub = kc_o.shape[0]

    def compress(x_ref, pos_ref, w1_ref, w2_ref):
        xf_ref[...] = x_ref[...].astype(F32)
        a = jnp.zeros((n_sub, CMP_HIDDEN), F32)
        b = jnp.zeros((n_sub, CMP_HIDDEN), F32)
        for tau in range(CMP_STRIDE):
            x_tau = xf_ref[pl.ds(tau, n_sub, stride=CMP_STRIDE), :]
            lo, hi = tau, CMP_STRIDE + tau
            a += _dot((x_tau + pos_ref[lo:lo + 1, :]).astype(BF16), w1_ref[lo])
            b += _dot((x_tau + pos_ref[hi:hi + 1, :]).astype(BF16), w1_ref[hi])
        hid = a + pltpu.roll(b, n_sub - 1, axis=0)
        hid = hid * jax.nn.sigmoid(hid)
        return _dot(hid.astype(BF16), w2_ref[...])

    kc_o[:, :HEAD_DIM] = _rms(compress(xk_ref, kpos_ref, kw1_ref, kw2_ref),
                              knorm_ref[0:1, :]).astype(BF16)
    kc_o[:, HEAD_DIM:] = _pos_features(n_sub, CMP_STRIDE, CMP_BLOCK - 1)
    vc_o[...] = compress(xv_ref, vpos_ref, vw1_ref, vw2_ref).astype(BF16)


def _nsa_prep(p3, k_norm, cmpk_pos, cmpk_w1, cmpk_w2, cmpv_pos, cmpv_w1, cmpv_w2, *, rows=2048):
    b, s, _ = p3.shape
    g = NSA_GROUPS
    n_sub = s // CMP_STRIDE
    kv = g * HEAD_DIM
    k_aug = 2 * HEAD_DIM

    full2 = lambda shp: pl.BlockSpec(shp, lambda i, j: (0,) * len(shp))
    seq_in = lambda c0: pl.BlockSpec((None, s, HEAD_DIM), lambda i, j: (i, 0, c0 // HEAD_DIM + j))
    cmp_io = lambda w: pl.BlockSpec((None, None, n_sub, w), lambda i, j: (i, j, 0, 0))
    cmp_shape = lambda w: jax.ShapeDtypeStruct((b, g, n_sub, w), BF16)
    w1_shape = (CMP_BLOCK, HEAD_DIM, CMP_HIDDEN)
    kc, vc = pl.pallas_call(
        _nsa_cmp_body,
        out_shape=(cmp_shape(k_aug), cmp_shape(HEAD_DIM)),
        grid=(b, g),
        in_specs=[seq_in(C_NKV), seq_in(C_NKV + kv),
                  full2((CMP_BLOCK, HEAD_DIM)), full2(w1_shape), full2((CMP_HIDDEN, HEAD_DIM)),
                  full2((CMP_BLOCK, HEAD_DIM)), full2(w1_shape), full2((CMP_HIDDEN, HEAD_DIM)),
                  full2((3, HEAD_DIM))],
        out_specs=(cmp_io(k_aug), cmp_io(HEAD_DIM)),
        scratch_shapes=[pltpu.VMEM((s, HEAD_DIM), F32)],
        compiler_params=_cparams(("parallel", "parallel")),
        name="nsa_cmp",
    )(p3, p3, cmpk_pos, cmpk_w1.reshape(w1_shape).astype(BF16), cmpk_w2.astype(BF16),
      cmpv_pos, cmpv_w1.reshape(w1_shape).astype(BF16), cmpv_w2.astype(BF16), k_norm)

    rows = min(rows, s)
    col = lambda c0: pl.BlockSpec((None, rows, HEAD_DIM),
                                  lambda i, j, t: (i, t, c0 // HEAD_DIM + j))
    key_io = lambda w: pl.BlockSpec((None, None, rows, w), lambda i, j, t: (i, j, t, 0))
    key_shape = lambda w: jax.ShapeDtypeStruct((b, g, s, w), BF16)
    ks, vs, kw, vw = pl.pallas_call(
        _nsa_keys_body,
        out_shape=(key_shape(k_aug), key_shape(HEAD_DIM), key_shape(k_aug), key_shape(HEAD_DIM)),
        grid=(b, g, s // rows),
        in_specs=[col(C_NKV + 2 * kv), col(C_NKV + 3 * kv), col(C_NKV + 4 * kv), col(C_NKV + 5 * kv),
                  pl.BlockSpec((3, HEAD_DIM), lambda i, j, t: (0, 0))],
        out_specs=(key_io(k_aug), key_io(HEAD_DIM), key_io(k_aug), key_io(HEAD_DIM)),
        compiler_params=_cparams(("parallel", "parallel", "parallel")),
        name="nsa_keys",
    )(p3, p3, p3, p3, k_norm)
    return kc, vc, ks, vs, kw, vw


def _split3(x):
    hi = x.astype(BF16).astype(F32)
    mid = (x - hi).astype(BF16).astype(F32)
    return hi, mid, x - hi - mid


def _nsa_body(q_ref, sm_ref, kc_ref, vc_ref, ks_ref, vs_ref, kw_ref, vw_ref,
              qn_ref, ovt_ref, e_ref, tg_ref, o_ref,
              m_ref, l_ref, acc_ref, part_ref, flag_ref, *, top_n):
    grp = pl.program_id(1)
    qb = pl.program_id(2)
    t0 = qb * Q_TILE
    nq = Q_TILE
    nr = NSA_HPG * nq
    ncp = kc_ref.shape[0]
    hrows = [slice(h * nq, (h + 1) * nq) for h in range(NSA_HPG)]
    hcols = [slice(h * HEAD_DIM, (h + 1) * HEAD_DIM) for h in range(NSA_HPG)]

    head = (lax.shift_right_logical(_iota((nr, 1), 0), Q_SHIFT) + grp * NSA_HPG + 1).astype(F32)
    slope = jnp.exp(head * (-ALIBI_MAX / NSA_HEADS * np.log(2.0))) * LOG2E
    lane = _iota((nr, LANES), 1)
    slope_feat = jnp.zeros((nr, LANES), F32)
    for i, term in enumerate(_split3(slope)):
        slope_feat = jnp.where(lane == i, term * float(1 << POS_SHIFT), slope_feat)
        slope_feat = jnp.where(lane == POS_SPLIT + i, term, slope_feat)
    qh = jnp.concatenate(
        [_rms(q_ref[:, hcols[h]].astype(F32), qn_ref[...]) * (HEAD_DIM ** -0.5 * LOG2E)
         for h in range(NSA_HPG)], axis=0)
    q = jnp.concatenate([qh, slope_feat], axis=1).astype(BF16)

    def softmax_rows(s, mask, row_ok=None):
        s = jnp.where(mask, s, NEG)
        e = jnp.exp2(s - jnp.max(s, axis=-1, keepdims=True))
        inv = 1.0 / jnp.sum(e, axis=-1, keepdims=True)
        return e * (inv if row_ok is None else jnp.where(row_ok, inv, 0.0))

    gates = jax.nn.sigmoid(sm_ref[...].astype(F32))

    def gate(branch, h):
        at = lambda gv: 2 * DN_HEADS + branch * NSA_HEADS + gv * NSA_HPG + h
        cols = [gates[:, at(gv):at(gv) + 1] for gv in range(NSA_GROUPS)]
        return jnp.where(grp == 0, cols[0], cols[1])

    s_c = _dot_nt(q, kc_ref[...])
    mask_c = (t0 + _iota((nq, ncp), 0)) >= (_iota((nq, ncp), 1) * CMP_STRIDE + (CMP_BLOCK - 1))
    sees_block = (t0 + _iota((nq, 1), 0)) >= CMP_BLOCK - 1
    p_c = [softmax_rows(s_c[hrows[h]], mask_c, sees_block) for h in range(NSA_HPG)]
    o_c = _dot(jnp.concatenate([p.astype(BF16) for p in p_c], axis=0), vc_ref[...])

    p_grp = p_c[0]
    for h in range(1, NSA_HPG):
        p_grp = p_grp + p_c[h]
    imp = _dot_nt(ovt_ref[...], p_grp, HIGHEST)
    blk = _iota((LANES, nq), 0)
    cur = lax.shift_right_logical(t0 + _iota((LANES, nq), 1), 6)
    forced = (blk == 0) | (blk == cur) | (blk == cur - 1)
    score = jnp.where(forced, -3e38, jnp.where(blk <= cur, imp, -1e6))
    sel_t = jnp.where(forced, 1.0, 0.0)
    blk_f = blk.astype(F32)

    wk = WINDOW + nq
    w0 = pl.multiple_of(jnp.maximum(t0 - WINDOW, 0), nq)
    s_w = _dot_nt(q, kw_ref[pl.ds(w0, wk), :])
    dist_w = (t0 + _iota((nq, wk), 0)) - (w0 + _iota((nq, wk), 1))
    mask_w = (dist_w >= 0) & (dist_w < WINDOW)
    p_w = []
    rounds = top_n - 3
    for r in range(rounds):
        best = jnp.max(score, axis=0, keepdims=True)
        first = jnp.min(jnp.where(score == best, blk_f, float(LANES)), axis=0, keepdims=True)
        pick = blk_f == first
        sel_t = jnp.where(pick, 1.0, sel_t)
        score = jnp.where(pick, -3e38, score)
        if (r + 1) * NSA_HPG // rounds > len(p_w):
            p_w.append(softmax_rows(s_w[hrows[len(p_w)]], mask_w).astype(BF16))
    eye = jnp.where(_iota((nq, nq), 0) == _iota((nq, nq), 1), 1.0, 0.0).astype(BF16)
    sel = _dot_nt(eye, sel_t.astype(BF16)).astype(BF16)
    o_w = _dot(jnp.concatenate(p_w, axis=0), vw_ref[pl.ds(w0, wk), :])
    for h in range(NSA_HPG):
        part_ref[:, hcols[h]] = gate(0, h) * o_c[hrows[h]] + gate(2, h) * o_w[hrows[h]]

    picked = jnp.broadcast_to(jnp.sum(sel.astype(F32), axis=0, keepdims=True), (8, LANES))
    tile_count = _dot(picked.astype(BF16), tg_ref[...])
    for t in range(flag_ref.shape[0]):
        flag_ref[t] = (tile_count[0, t] > 0.0).astype(jnp.int32)

    m_ref[...] = jnp.full(m_ref.shape, NEG, F32)
    l_ref[...] = jnp.zeros(l_ref.shape, F32)
    acc_ref[...] = jnp.zeros(acc_ref.shape, F32)
    kt_diag = lax.shift_right_logical(t0, 9)

    def sel_tile(kt, tiles, causal):
        width = tiles * SEL_KT
        k0 = pl.multiple_of(kt * SEL_KT, SEL_KT)
        s = _dot_nt(q, ks_ref[pl.ds(k0, width), :])
        mask = jnp.concatenate([_dot(sel, e_ref[kt + i]) for i in range(tiles)], axis=1) > 0.5
        if causal:
            mask = mask & ((t0 + _iota((nq, width), 0)) >= (k0 + _iota((nq, width), 1)))
        ps, alphas = [], []
        for h in range(NSA_HPG):
            sh = jnp.where(mask, s[hrows[h]], NEG)
            m_prev = m_ref[hrows[h], :]
            m_new = jnp.maximum(m_prev, jnp.max(sh, axis=-1, keepdims=True))
            alpha = jnp.exp2(m_prev - m_new)
            p = jnp.exp2(sh - m_new[:, 0:1])
            l_ref[hrows[h], :] = alpha * l_ref[hrows[h], :] + jnp.sum(p, axis=-1, keepdims=True)
            m_ref[hrows[h], :] = m_new
            ps.append(p.astype(BF16))
            alphas.append(alpha)
        pv = _dot(jnp.concatenate(ps, axis=0), vs_ref[pl.ds(k0, width), :])
        acc_ref[...] = jnp.concatenate(alphas, axis=0) * acc_ref[...] + pv

    kt_first = jnp.maximum(kt_diag - 1, 0)
    sel_tile(kt_first, 2, True)

    def earlier_tile(i, carry):
        kt = kt_first - 1 - i

        @pl.when(flag_ref[kt] > 0)
        def _():
            sel_tile(kt, 1, False)

        return carry

    lax.fori_loop(0, kt_first, earlier_tile, 0)
    o_s = acc_ref[...] / l_ref[...]
    gates = jax.nn.sigmoid(sm_ref[...].astype(F32))
    for h in range(NSA_HPG):
        o = part_ref[:, hcols[h]] + gate(1, h) * o_s[hrows[h]]
        o_ref[:, hcols[h]] = o.astype(o_ref.dtype)


def _nsa(p3, kc, vc, ks, vs, kw, vw, q_norm):
    b, s, _ = p3.shape
    g = NSA_GROUPS
    ncp = s // CMP_STRIDE
    n_sel = s // SEL_BLOCK
    assert n_sel <= LANES and s % SEL_KT == 0 and s >= WINDOW + Q_TILE and NSA_GROUPS == 2
    assert SEL_KT % Q_TILE == 0 and WINDOW % Q_TILE == 0 and s >= 2 * SEL_KT
    assert min(SEL_TOPN, n_sel) - 3 >= NSA_HPG
    gw = NSA_HPG * HEAD_DIM
    ci = np.arange(ncp)[:, None]
    sj = np.arange(LANES)[None, :]
    per = SEL_BLOCK // CMP_STRIDE
    overlap = ((ci // per == sj).astype(np.float32) + ((ci + 1) // per == sj).astype(np.float32))
    overlap[ncp - 1:] = 0.0
    key_blk = (np.arange(s) // SEL_BLOCK).reshape(s // SEL_KT, 1, SEL_KT)
    expand = (key_blk == np.arange(LANES)[None, :, None]).astype(np.float32)

    n_tiles = s // SEL_KT
    tile_of = (np.arange(LANES)[:, None] // (SEL_KT // SEL_BLOCK)
               == np.arange(LANES)[None, :]).astype(np.float32)

    kv = lambda rows, w: pl.BlockSpec((None, None, rows, w), lambda i, j, t: (i, j, 0, 0))
    full = lambda shp: pl.BlockSpec(shp, lambda i, j, t: (0,) * len(shp))
    return pl.pallas_call(
        functools.partial(_nsa_body, top_n=min(SEL_TOPN, n_sel)),
        out_shape=jax.ShapeDtypeStruct((b, s, NSA_WIDTH), BF16),
        grid=(b, g, s // Q_TILE),
        in_specs=[pl.BlockSpec((None, Q_TILE, gw), lambda i, j, t: (i, t, C_NQ // gw + j)),
                  pl.BlockSpec((None, Q_TILE, LANES), lambda i, j, t: (i, t, C_SMALL // LANES)),
                  kv(ncp, 2 * HEAD_DIM), kv(ncp, HEAD_DIM), kv(s, 2 * HEAD_DIM), kv(s, HEAD_DIM),
                  kv(s, 2 * HEAD_DIM), kv(s, HEAD_DIM),
                  full((1, HEAD_DIM)), full((LANES, ncp)), full((n_tiles, LANES, SEL_KT)),
                  full((LANES, LANES))],
        out_specs=pl.BlockSpec((None, Q_TILE, gw), lambda i, j, t: (i, t, j)),
        scratch_shapes=[pltpu.VMEM((NSA_HPG * Q_TILE, HEAD_DIM), F32)] * 3
        + [pltpu.VMEM((Q_TILE, gw), F32), pltpu.SMEM((n_tiles,), jnp.int32)],
        compiler_params=_cparams(("parallel", "parallel", "arbitrary")),
        name="nsa",
    )(p3, p3, kc, vc, ks, vs, kw, vw, q_norm.reshape(1, -1),
      jnp.asarray(overlap.T), jnp.asarray(expand, dtype=BF16), jnp.asarray(tile_of, dtype=BF16))


def kernel(x, ffn1_norm, ffn1_gate, ffn1_up, ffn1_down, mix_norm, w_in, dn_conv, dn_a_log,
           dn_dt_bias, dn_out_norm, nsa_q_norm, nsa_k_norm, cmpk_pos, cmpk_w1, cmpk_w2,
           cmpv_pos, cmpv_w1, cmpv_w2, sgu_norm_w, sgu_norm_b, sgu_w, sgu_b, w_branch_a,
           w_branch_b, w_branch_c, w_out, ffn2_norm, ffn2_gate, ffn2_up, ffn2_down):
    batch, seq, d = x.shape
    t = batch * seq
    ffn1 = _ffn_weights(ffn1_gate, ffn1_up, ffn1_down)
    ffn2 = _ffn_weights(ffn2_gate, ffn2_up, ffn2_down)
    w_in = _reorder_w_in(w_in)
    mix = tuple(w.astype(BF16) for w in (w_branch_a, w_branch_b, w_branch_c, w_out))

    h = x.reshape(t, d)
    for l in range(ffn1_norm.shape[0]):
        h = _ffn(h, ffn1_norm[l], *ffn1, l)
        p = _inproj(h, mix_norm[l], w_in, l)
        p3 = p.reshape(batch, seq, P_COLS)
        y_a = _deltanet(p3, dn_conv[l], dn_a_log[l], dn_dt_bias[l], dn_out_norm[l])
        kv = _nsa_prep(p3, nsa_k_norm[l], cmpk_pos[l], cmpk_w1[l], cmpk_w2[l],
                       cmpv_pos[l], cmpv_w1[l], cmpv_w2[l])
        y_b = _nsa(p3, *kv, nsa_q_norm[l])
        y_c = _sgu(p, sgu_norm_w[l], sgu_norm_b[l], sgu_w[l], sgu_b[l])
        h = _merge(h, p, y_a.reshape(t, -1), y_b.reshape(t, -1), y_c, *mix, l)
        h = _ffn(h, ffn2_norm[l], *ffn2, l)
    return h.reshape(batch, seq, d)
```

```python
import functools

import jax
import jax.numpy as jnp
import numpy as np
from jax import lax
from jax.experimental import pallas as pl
from jax.experimental.pallas import tpu as pltpu

F32 = jnp.float32
BF16 = jnp.bfloat16
HIGHEST = lax.Precision.HIGHEST

LANES = 128
V7X_VMEM_LIMIT = 56 * 1024 * 1024

HEAD_DIM = 128
EPS = 1e-6
NORM_ROWS = 256
NEG = -1e30

DN_HEADS = 8
DN_WIDTH = DN_HEADS * HEAD_DIM
DN_CONV = 4
DN_CHUNK = 64
DN_TILE = 2 * DN_CHUNK
DN_HALO = 8

NSA_HEADS = 8
NSA_GROUPS = 2
NSA_HPG = NSA_HEADS // NSA_GROUPS
NSA_WIDTH = NSA_HEADS * HEAD_DIM
CMP_STRIDE = 16
CMP_BLOCK = 2 * CMP_STRIDE
CMP_HIDDEN = 256
SEL_BLOCK = 64
SEL_TOPN = 16
WINDOW = 512
Q_SHIFT = 8
Q_TILE = 1 << Q_SHIFT
ALIBI_MAX = 8.0
SEL_KT = 512
NSA_ROW_CHUNK = Q_TILE
POS_SHIFT = 6
POS_SPLIT = 3
LOG2E = 1.4426950408889634

SGU_GROUPS = 8
SGU_WIDTH = SGU_GROUPS * HEAD_DIM
SGU_CHUNK = 128


def _cparams(sem):
    return pltpu.CompilerParams(dimension_semantics=sem, vmem_limit_bytes=V7X_VMEM_LIMIT)


def _rms(x, w):
    return x * lax.rsqrt(jnp.mean(x * x, axis=-1, keepdims=True) + EPS) * w


def _iota(shape, dim):
    return lax.broadcasted_iota(jnp.int32, shape, dim)


def _dot(a, b, precision=None):
    return jnp.dot(a, b, preferred_element_type=F32, precision=precision)


def _dot_nt(a, b, precision=None):
    return lax.dot_general(a, b, (((1,), (1,)), ((), ())), preferred_element_type=F32,
                           precision=precision)


def _ffn_body(x_ref, nw_ref, wg_ref, wu_ref, wd_ref, o_ref, h_ref):
    def half_swiglu(h):
        g = _dot(h, wg_ref[...])
        u = _dot(h, wu_ref[...])
        return _dot((0.5 * g * jax.nn.sigmoid(g) * u).astype(BF16), wd_ref[...])

    @pl.when(pl.program_id(1) == 0)
    def _():
        for r0 in range(0, x_ref.shape[0], NORM_ROWS):
            rows = slice(r0, r0 + NORM_ROWS)
            x = x_ref[rows, :]
            h = _rms(x, nw_ref[...]).astype(BF16)
            h_ref[rows, :] = h
            o_ref[rows, :] = x + half_swiglu(h)

    @pl.when(pl.program_id(1) > 0)
    def _():
        o_ref[...] += half_swiglu(h_ref[...])


FF_TILE = 512


def _pad_cols_body(x_ref, o_ref):
    n = x_ref.shape[1]
    o_ref[:, :n] = x_ref[...].astype(BF16)
    o_ref[:, n:] = jnp.zeros((o_ref.shape[0], o_ref.shape[1] - n), BF16)


def _pad_rows_body(x_ref, o_ref, *, n_blocks):
    inside = pl.program_id(1) < n_blocks
    o_ref[...] = jnp.where(inside, x_ref[...], 0.0).astype(BF16)


def _cast_pad(w, axis, extra, *, rows=128):
    nl, r, c = w.shape
    if axis == 2:
        return pl.pallas_call(
            _pad_cols_body,
            out_shape=jax.ShapeDtypeStruct((nl, r, c + extra), BF16),
            grid=(nl, r // rows),
            in_specs=[pl.BlockSpec((None, rows, c), lambda l, i: (l, i, 0))],
            out_specs=pl.BlockSpec((None, rows, c + extra), lambda l, i: (l, i, 0)),
            compiler_params=_cparams(("parallel", "parallel")),
            name="cast_pad_cols",
        )(w)
    assert r % rows == 0 and extra % rows == 0
    n_blocks = r // rows
    return pl.pallas_call(
        functools.partial(_pad_rows_body, n_blocks=n_blocks),
        out_shape=jax.ShapeDtypeStruct((nl, r + extra, c), BF16),
        grid=(nl, (r + extra) // rows),
        in_specs=[pl.BlockSpec((None, rows, c), lambda l, i: (l, jnp.minimum(i, n_blocks - 1), 0))],
        out_specs=pl.BlockSpec((None, rows, c), lambda l, i: (l, i, 0)),
        compiler_params=_cparams(("parallel", "parallel")),
        name="cast_pad_rows",
    )(w)


def _ffn_weights(w_gate, w_up, w_down):
    ff = w_gate.shape[-1]
    extra = FF_TILE * pl.cdiv(ff, FF_TILE) - ff
    return _cast_pad(w_gate, 2, extra), _cast_pad(w_up, 2, extra), _cast_pad(w_down, 1, extra)


def _ffn(x, norm_w, wg, wu, wd, layer, *, tm=1024, tf=FF_TILE):
    t, d = x.shape
    ffp = wg.shape[-1]
    return pl.pallas_call(
        _ffn_body,
        out_shape=jax.ShapeDtypeStruct((t, d), F32),
        grid=(t // tm, ffp // tf),
        in_specs=[
            pl.BlockSpec((tm, d), lambda i, j: (i, 0)),
            pl.BlockSpec((1, d), lambda i, j: (0, 0)),
            pl.BlockSpec((None, d, tf), lambda i, j: (layer, 0, j)),
            pl.BlockSpec((None, d, tf), lambda i, j: (layer, 0, j)),
            pl.BlockSpec((None, tf, d), lambda i, j: (layer, j, 0)),
        ],
        out_specs=pl.BlockSpec((tm, d), lambda i, j: (i, 0)),
        scratch_shapes=[pltpu.VMEM((tm, d), BF16)],
        compiler_params=_cparams(("parallel", "arbitrary")),
        name="ffn",
    )(x, norm_w.reshape(1, d), wg, wu, wd)


C_DQ, C_DK, C_DV, C_DZ = 0, 1024, 2048, 3072
C_NQ = 4096
C_SU, C_SV = 5120, 6144
C_GA, C_GB, C_GC = 7168, 9216, 11264
C_NKV = 13312
C_SMALL = 14848
P_COLS = 15360


W_IN_GROUPS = ((0, 4096),
               (4112, 1024),
               (6696, 2048),
               (8744, 6144),
               (5136, 1536))
W_IN_SMALL = ((4096, 16),
              (6672, 24))


def _reorder_body(x_ref, o_ref):
    rows = o_ref.shape[0]
    dst = 0
    for src, n in W_IN_GROUPS:
        o_ref[:, dst:dst + n] = x_ref[:, src:src + n].astype(BF16)
        dst += n
    small = [x_ref[:, src:src + n] for src, n in W_IN_SMALL]
    small.append(jnp.zeros((rows, LANES - sum(n for _, n in W_IN_SMALL)), F32))
    o_ref[:, dst:dst + LANES] = jnp.concatenate(small, axis=1).astype(BF16)
    o_ref[:, dst + LANES:] = jnp.zeros((rows, o_ref.shape[1] - dst - LANES), BF16)


def _reorder_w_in(w_in, *, rows=128):
    nl, d, n = w_in.shape
    assert sum(w for _, w in W_IN_GROUPS + W_IN_SMALL) == n
    return pl.pallas_call(
        _reorder_body,
        out_shape=jax.ShapeDtypeStruct((nl, d, P_COLS), BF16),
        grid=(nl, d // rows),
        in_specs=[pl.BlockSpec((None, rows, n), lambda l, i: (l, i, 0))],
        out_specs=pl.BlockSpec((None, rows, P_COLS), lambda l, i: (l, i, 0)),
        compiler_params=_cparams(("parallel", "parallel")),
        name="reorder_w_in",
    )(w_in)


def _inproj_body(x_ref, nw_ref, w_ref, o_ref, h_ref):
    @pl.when(pl.program_id(1) == 0)
    def _():
        for r0 in range(0, x_ref.shape[0], NORM_ROWS):
            rows = slice(r0, r0 + NORM_ROWS)
            h = _rms(x_ref[rows, :], nw_ref[...]).astype(BF16)
            h_ref[rows, :] = h
            o_ref[rows, :] = _dot(h, w_ref[...]).astype(o_ref.dtype)

    @pl.when(pl.program_id(1) > 0)
    def _():
        o_ref[...] = _dot(h_ref[...], w_ref[...]).astype(o_ref.dtype)


def _inproj(x, norm_w, w, layer, *, tm=1024, tn=1536):
    t, d = x.shape
    n = w.shape[-1]
    return pl.pallas_call(
        _inproj_body,
        out_shape=jax.ShapeDtypeStruct((t, n), BF16),
        grid=(t // tm, n // tn),
        in_specs=[
            pl.BlockSpec((tm, d), lambda i, j: (i, 0)),
            pl.BlockSpec((1, d), lambda i, j: (0, 0)),
            pl.BlockSpec((None, d, tn), lambda i, j: (layer, 0, j)),
        ],
        out_specs=pl.BlockSpec((tm, tn), lambda i, j: (i, j)),
        scratch_shapes=[pltpu.VMEM((tm, d), BF16)],
        compiler_params=_cparams(("parallel", "arbitrary")),
        name="inproj",
    )(x, norm_w.reshape(1, d), w)


def _merge_body(x_ref, ya_ref, yb_ref, yc_ref, ga_ref, gb_ref, gc_ref,
                wa_ref, wb_ref, wc_ref, wo_ref, o_ref):
    @pl.when(pl.program_id(1) == 0)
    def _():
        o_ref[...] = x_ref[...]

    m = jax.nn.sigmoid(ga_ref[...].astype(F32)) * _dot(ya_ref[...], wa_ref[...])
    m += jax.nn.sigmoid(gb_ref[...].astype(F32)) * _dot(yb_ref[...], wb_ref[...])
    m += jax.nn.sigmoid(gc_ref[...].astype(F32)) * _dot(yc_ref[...], wc_ref[...])
    o_ref[...] += _dot(m.astype(BF16), wo_ref[...])


def _merge(x, p, ya, yb, yc, wa, wb, wc, wo, layer, *, tm=512, tn=1024):
    t, d = x.shape
    k = ya.shape[1]
    y_spec = pl.BlockSpec((tm, k), lambda i, j: (i, 0))
    w_spec = pl.BlockSpec((None, k, tn), lambda i, j: (layer, 0, j))

    def gate_spec(col):
        assert col % tn == 0
        return pl.BlockSpec((tm, tn), lambda i, j: (i, col // tn + j))

    return pl.pallas_call(
        _merge_body,
        out_shape=jax.ShapeDtypeStruct((t, d), F32),
        grid=(t // tm, d // tn),
        in_specs=[pl.BlockSpec((tm, d), lambda i, j: (i, 0)), y_spec, y_spec, y_spec,
                  gate_spec(C_GA), gate_spec(C_GB), gate_spec(C_GC),
                  w_spec, w_spec, w_spec,
                  pl.BlockSpec((None, tn, d), lambda i, j: (layer, j, 0))],
        out_specs=pl.BlockSpec((tm, d), lambda i, j: (i, 0)),
        compiler_params=_cparams(("parallel", "arbitrary")),
        name="merge",
    )(x, ya, yb, yc, p, p, p, wa, wb, wc, wo)


def _sgu_body(u_ref, v_ref, nw_ref, nb_ref, ws_ref, bs_ref, o_ref, *, n_chunk):
    v = jax.nn.gelu(v_ref[...].astype(F32))
    mu = jnp.mean(v, axis=-1, keepdims=True)
    vc = v - mu
    var = jnp.mean(vc * vc, axis=-1, keepdims=True)
    vn = (vc * lax.rsqrt(var + EPS) * nw_ref[...] + nb_ref[...]).astype(BF16)
    tril = _iota((SGU_CHUNK, SGU_CHUNK), 0) >= _iota((SGU_CHUNK, SGU_CHUNK), 1)
    for g in range(SGU_GROUPS):
        cols = slice(g * HEAD_DIM, (g + 1) * HEAD_DIM)
        w = jnp.where(tril, ws_ref[g], 0.0).astype(BF16)
        rhs = jnp.concatenate(
            [vn[c * SGU_CHUNK:(c + 1) * SGU_CHUNK, cols] for c in range(n_chunk)], axis=1)
        s = _dot(w, rhs)
        for c in range(n_chunk):
            rows = slice(c * SGU_CHUNK, (c + 1) * SGU_CHUNK)
            u = jax.nn.gelu(u_ref[rows, cols].astype(F32))
            sc = s[:, c * HEAD_DIM:(c + 1) * HEAD_DIM] + bs_ref[g]
            o_ref[rows, cols] = (u * sc).astype(o_ref.dtype)


def _sgu(p, norm_w, norm_b, w_s, b_s, *, n_chunk=4):
    t = p.shape[0]
    ts = n_chunk * SGU_CHUNK
    bias = jnp.broadcast_to(b_s[:, :, None], (SGU_GROUPS, SGU_CHUNK, HEAD_DIM))
    return pl.pallas_call(
        functools.partial(_sgu_body, n_chunk=n_chunk),
        out_shape=jax.ShapeDtypeStruct((t, SGU_WIDTH), BF16),
        grid=(t // ts,),
        in_specs=[
            pl.BlockSpec((ts, SGU_WIDTH), lambda i: (i, C_SU // SGU_WIDTH)),
            pl.BlockSpec((ts, SGU_WIDTH), lambda i: (i, C_SV // SGU_WIDTH)),
            pl.BlockSpec((1, SGU_WIDTH), lambda i: (0, 0)),
            pl.BlockSpec((1, SGU_WIDTH), lambda i: (0, 0)),
            pl.BlockSpec((SGU_GROUPS, SGU_CHUNK, SGU_CHUNK), lambda i: (0, 0, 0)),
            pl.BlockSpec((SGU_GROUPS, SGU_CHUNK, HEAD_DIM), lambda i: (0, 0, 0)),
        ],
        out_specs=pl.BlockSpec((ts, SGU_WIDTH), lambda i: (i, 0)),
        compiler_params=_cparams(("parallel",)),
        name="sgu",
    )(p, p, norm_w.reshape(1, -1), norm_b.reshape(1, -1), w_s, bias)


def _softplus(x):
    return jnp.maximum(x, 0.0) + jnp.log(1.0 + jnp.exp(-jnp.abs(x)))


def _dn_body(q_ref, k_ref, v_ref, z_ref, sm_ref, conv_ref, alog_ref, dtb_ref, onorm_ref,
             o_ref, xbuf, s_ref):
    n = DN_TILE
    shape = (n, n)

    @pl.when(pl.program_id(1) == 0)
    def _():
        xbuf[0:DN_HALO, :] = jnp.zeros((DN_HALO, 3 * DN_WIDTH), F32)
        s_ref[...] = jnp.zeros_like(s_ref)

    xbuf[DN_HALO:, 0:DN_WIDTH] = q_ref[...].astype(F32)
    xbuf[DN_HALO:, DN_WIDTH:2 * DN_WIDTH] = k_ref[...].astype(F32)
    xbuf[DN_HALO:, 2 * DN_WIDTH:] = v_ref[...].astype(F32)

    def conv_silu(c0):
        cols = slice(c0, c0 + HEAD_DIM)
        acc = conv_ref[DN_CONV - 1:DN_CONV, cols] * xbuf[DN_HALO:DN_HALO + n, cols]
        for s in range(1, DN_CONV):
            acc += (conv_ref[DN_CONV - 1 - s:DN_CONV - s, cols]
                    * xbuf[DN_HALO - s:DN_HALO - s + n, cols])
        return acc * jax.nn.sigmoid(acc)

    row = _iota(shape, 0)
    col = _iota(shape, 1)
    same = lax.shift_right_logical(row, 6) == lax.shift_right_logical(col, 6)
    incl = same & (row >= col)
    strict = same & (row > col)
    first_chunk_cols = col < DN_CHUNK
    eye = jnp.where(row == col, 1.0, 0.0).astype(BF16)

    sm = sm_ref[...].astype(F32)
    g_all = -jnp.exp(alog_ref[...]) * _softplus(sm + dtb_ref[...])
    beta_all = jax.nn.sigmoid(sm)
    tri = jnp.where(incl, 1.0, 0.0)
    gc_all = _dot(tri, g_all, HIGHEST)
    last = jnp.where(col == jnp.where(row >= DN_CHUNK, n - 1, DN_CHUNK - 1), 1.0, 0.0)
    gl_all = _dot(last, gc_all, HIGHEST)
    gc_t = gc_all.T

    heads = range(DN_HEADS)
    c = DN_CHUNK
    lows, rhss, attns, q_decs, kd_t0s, kd_t1s, ends0, ends1 = [], [], [], [], [], [], [], []
    for h in heads:
        qa = conv_silu(h * HEAD_DIM)
        ka = conv_silu(DN_WIDTH + h * HEAD_DIM)
        va = conv_silu(2 * DN_WIDTH + h * HEAD_DIM)
        qn = qa * lax.rsqrt(jnp.sum(qa * qa, axis=-1, keepdims=True) + EPS) * HEAD_DIM ** -0.5
        kn = ka * lax.rsqrt(jnp.sum(ka * ka, axis=-1, keepdims=True) + EPS)
        beta = jnp.broadcast_to(beta_all[:, DN_HEADS + h:DN_HEADS + h + 1], shape)
        g_col = jnp.broadcast_to(gc_all[:, h:h + 1], shape)
        g_row = jnp.broadcast_to(gc_t[h:h + 1, :], shape)
        g_end = jnp.broadcast_to(gl_all[:, h:h + 1], shape)
        decay = jnp.where(incl, jnp.exp(jnp.where(incl, g_col - g_row, 0.0)), 0.0)
        e_col = jnp.exp(g_col)
        knb = kn.astype(BF16)
        lows.append(jnp.where(strict, beta * _dot_nt(knb, knb) * decay, 0.0))
        attns.append((_dot_nt(qn.astype(BF16), knb) * decay).astype(BF16))
        rhss.append(jnp.concatenate([va * beta, kn * beta * e_col], axis=1))
        q_decs.append((qn * e_col).astype(BF16))
        kd_t = _dot_nt(eye, (kn * jnp.exp(g_end - g_col)).astype(BF16))
        kd_t0s.append(jnp.where(first_chunk_cols, kd_t, 0.0).astype(BF16))
        kd_t1s.append(jnp.where(first_chunk_cols, 0.0, kd_t).astype(BF16))
        decay_end = jnp.exp(g_end)
        ends0.append(decay_end[0:1, :])
        ends1.append(decay_end[c:c + 1, :])

    ms = [(-low).astype(BF16) for low in lows]
    xs = [rhs + _dot(m, rhs.astype(BF16)) for m, rhs in zip(ms, rhss)]
    for _ in range(5):
        ms = [_dot(m, m).astype(BF16) for m in ms]
        xs = [x + _dot(m, x.astype(BF16)) for m, x in zip(ms, xs)]
    us = [x[:, :HEAD_DIM] for x in xs]
    ws = [x[:, HEAD_DIM:].astype(BF16) for x in xs]

    s0 = [s_ref[h] for h in heads]
    s0b = [s.astype(BF16) for s in s0]
    v0 = [us[h][:c] - _dot(ws[h][:c], s0b[h]) for h in heads]
    o0 = [_dot(q_decs[h][:c], s0b[h]) for h in heads]
    vf0 = [jnp.concatenate([v, jnp.zeros_like(v)], axis=0).astype(BF16) for v in v0]
    s1 = [s0[h] * ends0[h] + _dot(kd_t0s[h], vf0[h]) for h in heads]
    s1b = [s.astype(BF16) for s in s1]
    v1 = [us[h][c:] - _dot(ws[h][c:], s1b[h]) for h in heads]
    o1 = [_dot(q_decs[h][c:], s1b[h]) for h in heads]
    vf = [jnp.concatenate([v0[h], v1[h]], axis=0).astype(BF16) for h in heads]
    for h in heads:
        s_ref[h] = s1[h] * ends1[h] + _dot(kd_t1s[h], vf[h])
    for h in heads:
        hc = slice(h * HEAD_DIM, (h + 1) * HEAD_DIM)
        o = jnp.concatenate([o0[h], o1[h]], axis=0) + _dot(attns[h], vf[h])
        z = z_ref[:, hc].astype(F32)
        o_ref[:, hc] = (_rms(o, onorm_ref[...]) * (z * jax.nn.sigmoid(z))).astype(o_ref.dtype)

    xbuf[0:DN_HALO, :] = xbuf[n:n + DN_HALO, :]


def _deltanet(p3, conv_w, a_log, dt_bias, out_norm):
    b, s, _ = p3.shape
    wide = lambda c: pl.BlockSpec((None, DN_TILE, DN_WIDTH), lambda i, t: (i, t, c // DN_WIDTH))
    lane_pad = lambda a: jnp.pad(a.reshape(1, -1), ((0, 0), (0, LANES - a.shape[-1])))
    full = lambda shp: pl.BlockSpec(shp, lambda i, t: (0,) * len(shp))
    return pl.pallas_call(
        _dn_body,
        out_shape=jax.ShapeDtypeStruct((b, s, DN_WIDTH), BF16),
        grid=(b, s // DN_TILE),
        in_specs=[wide(C_DQ), wide(C_DK), wide(C_DV), wide(C_DZ),
                  pl.BlockSpec((None, DN_TILE, LANES), lambda i, t: (i, t, C_SMALL // LANES)),
                  full((DN_CONV, 3 * DN_WIDTH)), full((1, LANES)), full((1, LANES)),
                  full((1, HEAD_DIM))],
        out_specs=pl.BlockSpec((None, DN_TILE, DN_WIDTH), lambda i, t: (i, t, 0)),
        scratch_shapes=[pltpu.VMEM((DN_HALO + DN_TILE, 3 * DN_WIDTH), F32),
                        pltpu.VMEM((DN_HEADS, HEAD_DIM, HEAD_DIM), F32)],
        compiler_params=_cparams(("parallel", "arbitrary")),
        name="deltanet",
    )(p3, p3, p3, p3, p3, conv_w, lane_pad(a_log), lane_pad(dt_bias), out_norm.reshape(1, -1))


def _pos_features(n, stride, offset):
    pos = _iota((n, LANES), 0) * stride + offset
    lane = _iota((n, LANES), 1)
    feat = jnp.where(lane < POS_SPLIT, lax.shift_right_logical(pos, POS_SHIFT),
                     jnp.where(lane < 2 * POS_SPLIT, pos & ((1 << POS_SHIFT) - 1), 0))
    return feat.astype(F32).astype(BF16)


def _nsa_keys_body(ks_ref, vs_ref, kw_ref, vw_ref, knorm_ref, ks_o, vs_o, kw_o, vw_o):
    rows = ks_o.shape[0]
    feat = _pos_features(rows, 1, pl.program_id(2) * rows)
    ks_o[:, :HEAD_DIM] = _rms(ks_ref[...].astype(F32), knorm_ref[1:2, :]).astype(BF16)
    ks_o[:, HEAD_DIM:] = feat
    vs_o[...] = vs_ref[...]
    kw_o[:, :HEAD_DIM] = _rms(kw_ref[...].astype(F32), knorm_ref[2:3, :]).astype(BF16)
    kw_o[:, HEAD_DIM:] = feat
    vw_o[...] = vw_ref[...]


def _nsa_cmp_body(xk_ref, xv_ref, kpos_ref, kw1_ref, kw2_ref, vpos_ref, vw1_ref, vw2_ref,
                  knorm_ref, kc_o, vc_o, xf_ref):
    n_sub = kc_o.shape[0]

    def compress(x_ref, pos_ref, w1_ref, w2_ref):
        xf_ref[...] = x_ref[...].astype(F32)
        a = jnp.zeros((n_sub, CMP_HIDDEN), F32)
        b = jnp.zeros((n_sub, CMP_HIDDEN), F32)
        for tau in range(CMP_STRIDE):
            x_tau = xf_ref[pl.ds(tau, n_sub, stride=CMP_STRIDE), :]
            lo, hi = tau, CMP_STRIDE + tau
            a += _dot((x_tau + pos_ref[lo:lo + 1, :]).astype(BF16), w1_ref[lo])
            b += _dot((x_tau + pos_ref[hi:hi + 1, :]).astype(BF16), w1_ref[hi])
        hid = a + pltpu.roll(b, n_sub - 1, axis=0)
        hid = hid * jax.nn.sigmoid(hid)
        return _dot(hid.astype(BF16), w2_ref[...])

    kc_o[:, :HEAD_DIM] = _rms(compress(xk_ref, kpos_ref, kw1_ref, kw2_ref),
                              knorm_ref[0:1, :]).astype(BF16)
    kc_o[:, HEAD_DIM:] = _pos_features(n_sub, CMP_STRIDE, CMP_BLOCK - 1)
    vc_o[...] = compress(xv_ref, vpos_ref, vw1_ref, vw2_ref).astype(BF16)


def _nsa_prep(p3, k_norm, cmpk_pos, cmpk_w1, cmpk_w2, cmpv_pos, cmpv_w1, cmpv_w2, *, rows=2048):
    b, s, _ = p3.shape
    g = NSA_GROUPS
    n_sub = s // CMP_STRIDE
    kv = g * HEAD_DIM
    k_aug = 2 * HEAD_DIM

    full2 = lambda shp: pl.BlockSpec(shp, lambda i, j: (0,) * len(shp))
    seq_in = lambda c0: pl.BlockSpec((None, s, HEAD_DIM), lambda i, j: (i, 0, c0 // HEAD_DIM + j))
    cmp_io = lambda w: pl.BlockSpec((None, None, n_sub, w), lambda i, j: (i, j, 0, 0))
    cmp_shape = lambda w: jax.ShapeDtypeStruct((b, g, n_sub, w), BF16)
    w1_shape = (CMP_BLOCK, HEAD_DIM, CMP_HIDDEN)
    kc, vc = pl.pallas_call(
        _nsa_cmp_body,
        out_shape=(cmp_shape(k_aug), cmp_shape(HEAD_DIM)),
        grid=(b, g),
        in_specs=[seq_in(C_NKV), seq_in(C_NKV + kv),
                  full2((CMP_BLOCK, HEAD_DIM)), full2(w1_shape), full2((CMP_HIDDEN, HEAD_DIM)),
                  full2((CMP_BLOCK, HEAD_DIM)), full2(w1_shape), full2((CMP_HIDDEN, HEAD_DIM)),
                  full2((3, HEAD_DIM))],
        out_specs=(cmp_io(k_aug), cmp_io(HEAD_DIM)),
        scratch_shapes=[pltpu.VMEM((s, HEAD_DIM), F32)],
        compiler_params=_cparams(("parallel", "parallel")),
        name="nsa_cmp",
    )(p3, p3, cmpk_pos, cmpk_w1.reshape(w1_shape).astype(BF16), cmpk_w2.astype(BF16),
      cmpv_pos, cmpv_w1.reshape(w1_shape).astype(BF16), cmpv_w2.astype(BF16), k_norm)

    rows = min(rows, s)
    col = lambda c0: pl.BlockSpec((None, rows, HEAD_DIM),
                                  lambda i, j, t: (i, t, c0 // HEAD_DIM + j))
    key_io = lambda w: pl.BlockSpec((None, None, rows, w), lambda i, j, t: (i, j, t, 0))
    key_shape = lambda w: jax.ShapeDtypeStruct((b, g, s, w), BF16)
    ks, vs, kw, vw = pl.pallas_call(
        _nsa_keys_body,
        out_shape=(key_shape(k_aug), key_shape(HEAD_DIM), key_shape(k_aug), key_shape(HEAD_DIM)),
        grid=(b, g, s // rows),
        in_specs=[col(C_NKV + 2 * kv), col(C_NKV + 3 * kv), col(C_NKV + 4 * kv), col(C_NKV + 5 * kv),
                  pl.BlockSpec((3, HEAD_DIM), lambda i, j, t: (0, 0))],
        out_specs=(key_io(k_aug), key_io(HEAD_DIM), key_io(k_aug), key_io(HEAD_DIM)),
        compiler_params=_cparams(("parallel", "parallel", "parallel")),
        name="nsa_keys",
    )(p3, p3, p3, p3, k_norm)
    return kc, vc, ks, vs, kw, vw


def _split3(x):
    hi = x.astype(BF16).astype(F32)
    mid = (x - hi).astype(BF16).astype(F32)
    return hi, mid, x - hi - mid


def _nsa_body(q_ref, sm_ref, kc_ref, vc_ref, ks_ref, vs_ref, kw_ref, vw_ref,
              qn_ref, ovt_ref, e_ref, tg_ref, o_ref,
              m_ref, l_ref, acc_ref, part_ref, flag_ref, *, top_n):
    grp = pl.program_id(1)
    qb = pl.program_id(2)
    t0 = qb * Q_TILE
    nq = Q_TILE
    nr = NSA_HPG * nq
    ncp = kc_ref.shape[0]
    hrows = [slice(h * nq, (h + 1) * nq) for h in range(NSA_HPG)]
    hcols = [slice(h * HEAD_DIM, (h + 1) * HEAD_DIM) for h in range(NSA_HPG)]

    head = (lax.shift_right_logical(_iota((nr, 1), 0), Q_SHIFT) + grp * NSA_HPG + 1).astype(F32)
    slope = jnp.exp(head * (-ALIBI_MAX / NSA_HEADS * np.log(2.0))) * LOG2E
    lane = _iota((nr, LANES), 1)
    slope_feat = jnp.zeros((nr, LANES), F32)
    for i, term in enumerate(_split3(slope)):
        slope_feat = jnp.where(lane == i, term * float(1 << POS_SHIFT), slope_feat)
        slope_feat = jnp.where(lane == POS_SPLIT + i, term, slope_feat)
    qh = jnp.concatenate(
        [_rms(q_ref[:, hcols[h]].astype(F32), qn_ref[...]) * (HEAD_DIM ** -0.5 * LOG2E)
         for h in range(NSA_HPG)], axis=0)
    q = jnp.concatenate([qh, slope_feat], axis=1).astype(BF16)

    rc = NSA_ROW_CHUNK
    row_chunks = [(h * nq + q0, q0) for q0 in range(0, nq, rc) for h in range(NSA_HPG)]

    def stacked(parts):
        return jnp.concatenate([parts[r0] for r0 in sorted(parts)], axis=0)

    def exp_chunk(s, bias, r0, q0):
        sc = s[r0:r0 + rc] + bias[q0:q0 + rc]
        e = jnp.exp2(sc - jnp.max(sc, axis=-1, keepdims=True))
        return e, jnp.sum(e, axis=-1, keepdims=True)

    gates = jax.nn.sigmoid(sm_ref[...].astype(F32))

    def gate(branch, h):
        at = lambda gv: 2 * DN_HEADS + branch * NSA_HEADS + gv * NSA_HPG + h
        cols = [gates[:, at(gv):at(gv) + 1] for gv in range(NSA_GROUPS)]
        return jnp.where(grp == 0, cols[0], cols[1])

    s_c = _dot_nt(q, kc_ref[...])
    mask_c = (t0 + _iota((nq, ncp), 0)) >= (_iota((nq, ncp), 1) * CMP_STRIDE + (CMP_BLOCK - 1))
    bias_c = jnp.where(mask_c, 0.0, NEG)
    sees_block = (t0 + _iota((nq, 1), 0)) >= CMP_BLOCK - 1
    e_c, inv_c, grp_c = {}, {}, {}
    for r0, q0 in row_chunks:
        e, total = exp_chunk(s_c, bias_c, r0, q0)
        inv_c[r0] = jnp.where(sees_block[q0:q0 + rc], 1.0 / total, 0.0)
        e_c[r0] = e.astype(BF16)
        grp_c[q0] = e * inv_c[r0] + (grp_c[q0] if q0 in grp_c else 0.0)
    o_c = _dot(stacked(e_c), vc_ref[...]) * stacked(inv_c)
    p_grp = stacked(grp_c)

    imp = _dot_nt(ovt_ref[...], p_grp, HIGHEST)
    blk = _iota((LANES, nq), 0)
    cur = lax.shift_right_logical(t0 + _iota((LANES, nq), 1), 6)
    forced = (blk == 0) | (blk == cur) | (blk == cur - 1)
    score = jnp.where(forced, -3e38, jnp.where(blk <= cur, imp, -1e6))
    sel_t = jnp.where(forced, 1.0, 0.0)
    blk_f = blk.astype(F32)

    wk = WINDOW + nq
    w0 = pl.multiple_of(jnp.maximum(t0 - WINDOW, 0), nq)
    s_w = _dot_nt(q, kw_ref[pl.ds(w0, wk), :])
    dist_w = (t0 + _iota((nq, wk), 0)) - (w0 + _iota((nq, wk), 1))
    bias_w = jnp.where((dist_w >= 0) & (dist_w < WINDOW), 0.0, NEG)
    e_w, inv_w = {}, {}
    rounds = top_n - 3
    for r in range(rounds):
        best = jnp.max(score, axis=0, keepdims=True)
        first = jnp.min(jnp.where(score == best, blk_f, float(LANES)), axis=0, keepdims=True)
        pick = blk_f == first
        sel_t = jnp.where(pick, 1.0, sel_t)
        score = jnp.where(pick, -3e38, score)
        for r0, q0 in row_chunks[len(e_w):(r + 1) * len(row_chunks) // rounds]:
            e, total = exp_chunk(s_w, bias_w, r0, q0)
            e_w[r0], inv_w[r0] = e.astype(BF16), 1.0 / total
    eye = jnp.where(_iota((nq, nq), 0) == _iota((nq, nq), 1), 1.0, 0.0).astype(BF16)
    sel = _dot_nt(eye, sel_t.astype(BF16)).astype(BF16)
    o_w = _dot(stacked(e_w), vw_ref[pl.ds(w0, wk), :]) * stacked(inv_w)
    for h in range(NSA_HPG):
        part_ref[:, hcols[h]] = gate(0, h) * o_c[hrows[h]] + gate(2, h) * o_w[hrows[h]]

    picked = jnp.broadcast_to(jnp.sum(sel.astype(F32), axis=0, keepdims=True), (8, LANES))
    tile_count = _dot(picked.astype(BF16), tg_ref[...])
    for t in range(flag_ref.shape[0]):
        flag_ref[t] = (tile_count[0, t] > 0.0).astype(jnp.int32)

    m_ref[...] = jnp.full(m_ref.shape, NEG, F32)
    l_ref[...] = jnp.zeros(l_ref.shape, F32)
    acc_ref[...] = jnp.zeros(acc_ref.shape, F32)
    kt_diag = lax.shift_right_logical(t0, 9)

    def sel_tile(kt, tiles, causal):
        width = tiles * SEL_KT
        k0 = pl.multiple_of(kt * SEL_KT, SEL_KT)
        s = _dot_nt(q, ks_ref[pl.ds(k0, width), :])
        mask = jnp.concatenate([_dot(sel, e_ref[kt + i]) for i in range(tiles)], axis=1) > 0.5
        if causal:
            mask = mask & ((t0 + _iota((nq, width), 0)) >= (k0 + _iota((nq, width), 1)))
        bias = jnp.where(mask, 0.0, NEG)
        es, alphas = {}, {}
        for r0, q0 in row_chunks:
            rows = slice(r0, r0 + rc)
            sc = s[rows] + bias[q0:q0 + rc]
            m_prev = m_ref[rows, :]
            m_new = jnp.maximum(m_prev, jnp.max(sc, axis=-1, keepdims=True))
            alphas[r0] = jnp.exp2(m_prev - m_new)
            e = jnp.exp2(sc - m_new[:, 0:1])
            l_ref[rows, :] = alphas[r0] * l_ref[rows, :] + jnp.sum(e, axis=-1, keepdims=True)
            m_ref[rows, :] = m_new
            es[r0] = e.astype(BF16)
        pv = _dot(stacked(es), vs_ref[pl.ds(k0, width), :])
        acc_ref[...] = stacked(alphas) * acc_ref[...] + pv

    kt_first = jnp.maximum(kt_diag - 1, 0)
    sel_tile(kt_first, 2, True)

    def earlier_tile(i, carry):
        kt = kt_first - 1 - i

        @pl.when(flag_ref[kt] > 0)
        def _():
            sel_tile(kt, 1, False)

        return carry

    lax.fori_loop(0, kt_first, earlier_tile, 0)
    o_s = acc_ref[...] / l_ref[...]
    gates = jax.nn.sigmoid(sm_ref[...].astype(F32))
    for h in range(NSA_HPG):
        o = part_ref[:, hcols[h]] + gate(1, h) * o_s[hrows[h]]
        o_ref[:, hcols[h]] = o.astype(o_ref.dtype)


def _nsa(p3, kc, vc, ks, vs, kw, vw, q_norm):
    b, s, _ = p3.shape
    g = NSA_GROUPS
    ncp = s // CMP_STRIDE
    n_sel = s // SEL_BLOCK
    assert n_sel <= LANES and s % SEL_KT == 0 and s >= WINDOW + Q_TILE and NSA_GROUPS == 2
    assert SEL_KT % Q_TILE == 0 and WINDOW % Q_TILE == 0 and s >= 2 * SEL_KT
    assert min(SEL_TOPN, n_sel) - 3 >= NSA_HPG
    gw = NSA_HPG * HEAD_DIM
    ci = np.arange(ncp)[:, None]
    sj = np.arange(LANES)[None, :]
    per = SEL_BLOCK // CMP_STRIDE
    overlap = ((ci // per == sj).astype(np.float32) + ((ci + 1) // per == sj).astype(np.float32))
    overlap[ncp - 1:] = 0.0
    key_blk = (np.arange(s) // SEL_BLOCK).reshape(s // SEL_KT, 1, SEL_KT)
    expand = (key_blk == np.arange(LANES)[None, :, None]).astype(np.float32)

    n_tiles = s // SEL_KT
    tile_of = (np.arange(LANES)[:, None] // (SEL_KT // SEL_BLOCK)
               == np.arange(LANES)[None, :]).astype(np.float32)

    kv = lambda rows, w: pl.BlockSpec((None, None, rows, w), lambda i, j, t: (i, j, 0, 0))
    full = lambda shp: pl.BlockSpec(shp, lambda i, j, t: (0,) * len(shp))
    return pl.pallas_call(
        functools.partial(_nsa_body, top_n=min(SEL_TOPN, n_sel)),
        out_shape=jax.ShapeDtypeStruct((b, s, NSA_WIDTH), BF16),
        grid=(b, g, s // Q_TILE),
        in_specs=[pl.BlockSpec((None, Q_TILE, gw), lambda i, j, t: (i, t, C_NQ // gw + j)),
                  pl.BlockSpec((None, Q_TILE, LANES), lambda i, j, t: (i, t, C_SMALL // LANES)),
                  kv(ncp, 2 * HEAD_DIM), kv(ncp, HEAD_DIM), kv(s, 2 * HEAD_DIM), kv(s, HEAD_DIM),
                  kv(s, 2 * HEAD_DIM), kv(s, HEAD_DIM),
                  full((1, HEAD_DIM)), full((LANES, ncp)), full((n_tiles, LANES, SEL_KT)),
                  full((LANES, LANES))],
        out_specs=pl.BlockSpec((None, Q_TILE, gw), lambda i, j, t: (i, t, j)),
        scratch_shapes=[pltpu.VMEM((NSA_HPG * Q_TILE, HEAD_DIM), F32)] * 3
        + [pltpu.VMEM((Q_TILE, gw), F32), pltpu.SMEM((n_tiles,), jnp.int32)],
        compiler_params=_cparams(("parallel", "parallel", "arbitrary")),
        name="nsa",
    )(p3, p3, kc, vc, ks, vs, kw, vw, q_norm.reshape(1, -1),
      jnp.asarray(overlap.T), jnp.asarray(expand, dtype=BF16), jnp.asarray(tile_of, dtype=BF16))


def kernel(x, ffn1_norm, ffn1_gate, ffn1_up, ffn1_down, mix_norm, w_in, dn_conv, dn_a_log,
           dn_dt_bias, dn_out_norm, nsa_q_norm, nsa_k_norm, cmpk_pos, cmpk_w1, cmpk_w2,
           cmpv_pos, cmpv_w1, cmpv_w2, sgu_norm_w, sgu_norm_b, sgu_w, sgu_b, w_branch_a,
           w_branch_b, w_branch_c, w_out, ffn2_norm, ffn2_gate, ffn2_up, ffn2_down):
    batch, seq, d = x.shape
    t = batch * seq
    ffn1 = _ffn_weights(ffn1_gate, ffn1_up, ffn1_down)
    ffn2 = _ffn_weights(ffn2_gate, ffn2_up, ffn2_down)
    w_in = _reorder_w_in(w_in)
    mix = tuple(w.astype(BF16) for w in (w_branch_a, w_branch_b, w_branch_c, w_out))

    h = x.reshape(t, d)
    for l in range(ffn1_norm.shape[0]):
        h = _ffn(h, ffn1_norm[l], *ffn1, l)
        p = _inproj(h, mix_norm[l], w_in, l)
        p3 = p.reshape(batch, seq, P_COLS)
        y_a = _deltanet(p3, dn_conv[l], dn_a_log[l], dn_dt_bias[l], dn_out_norm[l])
        kv = _nsa_prep(p3, nsa_k_norm[l], cmpk_pos[l], cmpk_w1[l], cmpk_w2[l],
                       cmpv_pos[l], cmpv_w1[l], cmpv_w2[l])
        y_b = _nsa(p3, *kv, nsa_q_norm[l])
        y_c = _sgu(p, sgu_norm_w[l], sgu_norm_b[l], sgu_w[l], sgu_b[l])
        h = _merge(h, p, y_a.reshape(t, -1), y_b.reshape(t, -1), y_c, *mix, l)
        h = _ffn(h, ffn2_norm[l], *ffn2, l)
    return h.reshape(batch, seq, d)
```

```python
import functools

import jax
import jax.numpy as jnp
import numpy as np
from jax import lax
from jax.experimental import pallas as pl
from jax.experimental.pallas import tpu as pltpu

F32 = jnp.float32
BF16 = jnp.bfloat16
HIGHEST = lax.Precision.HIGHEST

LANES = 128
V7X_VMEM_LIMIT = 56 * 1024 * 1024

HEAD_DIM = 128
EPS = 1e-6
NORM_ROWS = 256
NEG = -1e30

DN_HEADS = 8
DN_WIDTH = DN_HEADS * HEAD_DIM
DN_CONV = 4
DN_CHUNK = 64
DN_TILE = 2 * DN_CHUNK
DN_HALO = 8
DN_STEP_TILES = 2

NSA_HEADS = 8
NSA_GROUPS = 2
NSA_HPG = NSA_HEADS // NSA_GROUPS
NSA_WIDTH = NSA_HEADS * HEAD_DIM
CMP_STRIDE = 16
CMP_BLOCK = 2 * CMP_STRIDE
CMP_HIDDEN = 256
SEL_BLOCK = 64
SEL_TOPN = 16
WINDOW = 512
Q_SHIFT = 8
Q_TILE = 1 << Q_SHIFT
ALIBI_MAX = 8.0
SEL_KT = 512
NSA_ROW_CHUNK = Q_TILE
POS_SHIFT = 6
POS_SPLIT = 3
LOG2E = 1.4426950408889634

SGU_GROUPS = 8
SGU_WIDTH = SGU_GROUPS * HEAD_DIM
SGU_CHUNK = 128


def _cparams(sem):
    return pltpu.CompilerParams(dimension_semantics=sem, vmem_limit_bytes=V7X_VMEM_LIMIT)


def _rms(x, w):
    return x * lax.rsqrt(jnp.mean(x * x, axis=-1, keepdims=True) + EPS) * w


def _iota(shape, dim):
    return lax.broadcasted_iota(jnp.int32, shape, dim)


def _dot(a, b, precision=None):
    return jnp.dot(a, b, preferred_element_type=F32, precision=precision)


def _dot_nt(a, b, precision=None):
    return lax.dot_general(a, b, (((1,), (1,)), ((), ())), preferred_element_type=F32,
                           precision=precision)


def _ffn_body(x_ref, nw_ref, wg_ref, wu_ref, wd_ref, o_ref, h_ref):
    def half_swiglu(h):
        g = _dot(h, wg_ref[...])
        u = _dot(h, wu_ref[...])
        return _dot((0.5 * g * jax.nn.sigmoid(g) * u).astype(BF16), wd_ref[...])

    @pl.when(pl.program_id(1) == 0)
    def _():
        for r0 in range(0, x_ref.shape[0], NORM_ROWS):
            rows = slice(r0, r0 + NORM_ROWS)
            x = x_ref[rows, :]
            h = _rms(x, nw_ref[...]).astype(BF16)
            h_ref[rows, :] = h
            o_ref[rows, :] = x + half_swiglu(h)

    @pl.when(pl.program_id(1) > 0)
    def _():
        o_ref[...] += half_swiglu(h_ref[...])


FF_TILE = 512


def _pad_cols_body(x_ref, o_ref):
    n = x_ref.shape[1]
    o_ref[:, :n] = x_ref[...].astype(BF16)
    o_ref[:, n:] = jnp.zeros((o_ref.shape[0], o_ref.shape[1] - n), BF16)


def _pad_rows_body(x_ref, o_ref, *, n_blocks):
    inside = pl.program_id(1) < n_blocks
    o_ref[...] = jnp.where(inside, x_ref[...], 0.0).astype(BF16)


def _cast_pad(w, axis, extra, *, rows=128):
    nl, r, c = w.shape
    if axis == 2:
        return pl.pallas_call(
            _pad_cols_body,
            out_shape=jax.ShapeDtypeStruct((nl, r, c + extra), BF16),
            grid=(nl, r // rows),
            in_specs=[pl.BlockSpec((None, rows, c), lambda l, i: (l, i, 0))],
            out_specs=pl.BlockSpec((None, rows, c + extra), lambda l, i: (l, i, 0)),
            compiler_params=_cparams(("parallel", "parallel")),
            name="cast_pad_cols",
        )(w)
    assert r % rows == 0 and extra % rows == 0
    n_blocks = r // rows
    return pl.pallas_call(
        functools.partial(_pad_rows_body, n_blocks=n_blocks),
        out_shape=jax.ShapeDtypeStruct((nl, r + extra, c), BF16),
        grid=(nl, (r + extra) // rows),
        in_specs=[pl.BlockSpec((None, rows, c), lambda l, i: (l, jnp.minimum(i, n_blocks - 1), 0))],
        out_specs=pl.BlockSpec((None, rows, c), lambda l, i: (l, i, 0)),
        compiler_params=_cparams(("parallel", "parallel")),
        name="cast_pad_rows",
    )(w)


def _ffn_weights(w_gate, w_up, w_down):
    ff = w_gate.shape[-1]
    extra = FF_TILE * pl.cdiv(ff, FF_TILE) - ff
    return _cast_pad(w_gate, 2, extra), _cast_pad(w_up, 2, extra), _cast_pad(w_down, 1, extra)


def _ffn(x, norm_w, wg, wu, wd, layer, *, tm=1024, tf=FF_TILE):
    t, d = x.shape
    ffp = wg.shape[-1]
    return pl.pallas_call(
        _ffn_body,
        out_shape=jax.ShapeDtypeStruct((t, d), F32),
        grid=(t // tm, ffp // tf),
        in_specs=[
            pl.BlockSpec((tm, d), lambda i, j: (i, 0)),
            pl.BlockSpec((1, d), lambda i, j: (0, 0)),
            pl.BlockSpec((None, d, tf), lambda i, j: (layer, 0, j)),
            pl.BlockSpec((None, d, tf), lambda i, j: (layer, 0, j)),
            pl.BlockSpec((None, tf, d), lambda i, j: (layer, j, 0)),
        ],
        out_specs=pl.BlockSpec((tm, d), lambda i, j: (i, 0)),
        scratch_shapes=[pltpu.VMEM((tm, d), BF16)],
        compiler_params=_cparams(("parallel", "arbitrary")),
        name="ffn",
    )(x, norm_w.reshape(1, d), wg, wu, wd)


C_DQ, C_DK, C_DV, C_DZ = 0, 1024, 2048, 3072
C_NQ = 4096
C_SU, C_SV = 5120, 6144
C_GA, C_GB, C_GC = 7168, 9216, 11264
C_NKV = 13312
C_SMALL = 14848
P_COLS = 15360


W_IN_GROUPS = ((0, 4096),
               (4112, 1024),
               (6696, 2048),
               (8744, 6144),
               (5136, 1536))
W_IN_SMALL = ((4096, 16),
              (6672, 24))


def _reorder_body(x_ref, o_ref):
    rows = o_ref.shape[0]
    dst = 0
    for src, n in W_IN_GROUPS:
        o_ref[:, dst:dst + n] = x_ref[:, src:src + n].astype(BF16)
        dst += n
    small = [x_ref[:, src:src + n] for src, n in W_IN_SMALL]
    small.append(jnp.zeros((rows, LANES - sum(n for _, n in W_IN_SMALL)), F32))
    o_ref[:, dst:dst + LANES] = jnp.concatenate(small, axis=1).astype(BF16)
    o_ref[:, dst + LANES:] = jnp.zeros((rows, o_ref.shape[1] - dst - LANES), BF16)


def _reorder_w_in(w_in, *, rows=128):
    nl, d, n = w_in.shape
    assert sum(w for _, w in W_IN_GROUPS + W_IN_SMALL) == n
    return pl.pallas_call(
        _reorder_body,
        out_shape=jax.ShapeDtypeStruct((nl, d, P_COLS), BF16),
        grid=(nl, d // rows),
        in_specs=[pl.BlockSpec((None, rows, n), lambda l, i: (l, i, 0))],
        out_specs=pl.BlockSpec((None, rows, P_COLS), lambda l, i: (l, i, 0)),
        compiler_params=_cparams(("parallel", "parallel")),
        name="reorder_w_in",
    )(w_in)


def _inproj_body(x_ref, nw_ref, w_ref, o_ref, h_ref):
    @pl.when(pl.program_id(1) == 0)
    def _():
        for r0 in range(0, x_ref.shape[0], NORM_ROWS):
            rows = slice(r0, r0 + NORM_ROWS)
            h = _rms(x_ref[rows, :], nw_ref[...]).astype(BF16)
            h_ref[rows, :] = h
            o_ref[rows, :] = _dot(h, w_ref[...]).astype(o_ref.dtype)

    @pl.when(pl.program_id(1) > 0)
    def _():
        o_ref[...] = _dot(h_ref[...], w_ref[...]).astype(o_ref.dtype)


def _inproj(x, norm_w, w, layer, *, tm=1024, tn=1536):
    t, d = x.shape
    n = w.shape[-1]
    return pl.pallas_call(
        _inproj_body,
        out_shape=jax.ShapeDtypeStruct((t, n), BF16),
        grid=(t // tm, n // tn),
        in_specs=[
            pl.BlockSpec((tm, d), lambda i, j: (i, 0)),
            pl.BlockSpec((1, d), lambda i, j: (0, 0)),
            pl.BlockSpec((None, d, tn), lambda i, j: (layer, 0, j)),
        ],
        out_specs=pl.BlockSpec((tm, tn), lambda i, j: (i, j)),
        scratch_shapes=[pltpu.VMEM((tm, d), BF16)],
        compiler_params=_cparams(("parallel", "arbitrary")),
        name="inproj",
    )(x, norm_w.reshape(1, d), w)


def _merge_body(x_ref, ya_ref, yb_ref, yc_ref, ga_ref, gb_ref, gc_ref,
                wa_ref, wb_ref, wc_ref, wo_ref, o_ref):
    @pl.when(pl.program_id(1) == 0)
    def _():
        o_ref[...] = x_ref[...]

    m = jax.nn.sigmoid(ga_ref[...].astype(F32)) * _dot(ya_ref[...], wa_ref[...])
    m += jax.nn.sigmoid(gb_ref[...].astype(F32)) * _dot(yb_ref[...], wb_ref[...])
    m += jax.nn.sigmoid(gc_ref[...].astype(F32)) * _dot(yc_ref[...], wc_ref[...])
    o_ref[...] += _dot(m.astype(BF16), wo_ref[...])


def _merge(x, p, ya, yb, yc, wa, wb, wc, wo, layer, *, tm=512, tn=1024):
    t, d = x.shape
    k = ya.shape[1]
    y_spec = pl.BlockSpec((tm, k), lambda i, j: (i, 0))
    w_spec = pl.BlockSpec((None, k, tn), lambda i, j: (layer, 0, j))

    def gate_spec(col):
        assert col % tn == 0
        return pl.BlockSpec((tm, tn), lambda i, j: (i, col // tn + j))

    return pl.pallas_call(
        _merge_body,
        out_shape=jax.ShapeDtypeStruct((t, d), F32),
        grid=(t // tm, d // tn),
        in_specs=[pl.BlockSpec((tm, d), lambda i, j: (i, 0)), y_spec, y_spec, y_spec,
                  gate_spec(C_GA), gate_spec(C_GB), gate_spec(C_GC),
                  w_spec, w_spec, w_spec,
                  pl.BlockSpec((None, tn, d), lambda i, j: (layer, j, 0))],
        out_specs=pl.BlockSpec((tm, d), lambda i, j: (i, 0)),
        compiler_params=_cparams(("parallel", "arbitrary")),
        name="merge",
    )(x, ya, yb, yc, p, p, p, wa, wb, wc, wo)


def _sgu_body(u_ref, v_ref, nw_ref, nb_ref, ws_ref, bs_ref, o_ref, *, n_chunk):
    v = jax.nn.gelu(v_ref[...].astype(F32))
    mu = jnp.mean(v, axis=-1, keepdims=True)
    vc = v - mu
    var = jnp.mean(vc * vc, axis=-1, keepdims=True)
    vn = (vc * lax.rsqrt(var + EPS) * nw_ref[...] + nb_ref[...]).astype(BF16)
    tril = _iota((SGU_CHUNK, SGU_CHUNK), 0) >= _iota((SGU_CHUNK, SGU_CHUNK), 1)
    for g in range(SGU_GROUPS):
        cols = slice(g * HEAD_DIM, (g + 1) * HEAD_DIM)
        w = jnp.where(tril, ws_ref[g], 0.0).astype(BF16)
        rhs = jnp.concatenate(
            [vn[c * SGU_CHUNK:(c + 1) * SGU_CHUNK, cols] for c in range(n_chunk)], axis=1)
        s = _dot(w, rhs)
        for c in range(n_chunk):
            rows = slice(c * SGU_CHUNK, (c + 1) * SGU_CHUNK)
            u = jax.nn.gelu(u_ref[rows, cols].astype(F32))
            sc = s[:, c * HEAD_DIM:(c + 1) * HEAD_DIM] + bs_ref[g]
            o_ref[rows, cols] = (u * sc).astype(o_ref.dtype)


def _sgu(p, norm_w, norm_b, w_s, b_s, *, n_chunk=4):
    t = p.shape[0]
    ts = n_chunk * SGU_CHUNK
    bias = jnp.broadcast_to(b_s[:, :, None], (SGU_GROUPS, SGU_CHUNK, HEAD_DIM))
    return pl.pallas_call(
        functools.partial(_sgu_body, n_chunk=n_chunk),
        out_shape=jax.ShapeDtypeStruct((t, SGU_WIDTH), BF16),
        grid=(t // ts,),
        in_specs=[
            pl.BlockSpec((ts, SGU_WIDTH), lambda i: (i, C_SU // SGU_WIDTH)),
            pl.BlockSpec((ts, SGU_WIDTH), lambda i: (i, C_SV // SGU_WIDTH)),
            pl.BlockSpec((1, SGU_WIDTH), lambda i: (0, 0)),
            pl.BlockSpec((1, SGU_WIDTH), lambda i: (0, 0)),
            pl.BlockSpec((SGU_GROUPS, SGU_CHUNK, SGU_CHUNK), lambda i: (0, 0, 0)),
            pl.BlockSpec((SGU_GROUPS, SGU_CHUNK, HEAD_DIM), lambda i: (0, 0, 0)),
        ],
        out_specs=pl.BlockSpec((ts, SGU_WIDTH), lambda i: (i, 0)),
        compiler_params=_cparams(("parallel",)),
        name="sgu",
    )(p, p, norm_w.reshape(1, -1), norm_b.reshape(1, -1), w_s, bias)


def _softplus(x):
    return jnp.maximum(x, 0.0) + jnp.log(1.0 + jnp.exp(-jnp.abs(x)))


def _dn_body(q_ref, k_ref, v_ref, z_ref, sm_ref, conv_ref, alog_ref, dtb_ref, onorm_ref,
             o_ref, xbuf, s_ref):
    n = DN_TILE
    shape = (n, n)
    n_tiles = q_ref.shape[0] // n

    @pl.when(pl.program_id(1) == 0)
    def _():
        xbuf[0:DN_HALO, :] = jnp.zeros((DN_HALO, 3 * DN_WIDTH), F32)
        s_ref[...] = jnp.zeros_like(s_ref)

    xbuf[DN_HALO:, 0:DN_WIDTH] = q_ref[...].astype(F32)
    xbuf[DN_HALO:, DN_WIDTH:2 * DN_WIDTH] = k_ref[...].astype(F32)
    xbuf[DN_HALO:, 2 * DN_WIDTH:] = v_ref[...].astype(F32)

    def conv_silu(r0, c0):
        cols = slice(c0, c0 + HEAD_DIM)
        top = DN_HALO + r0
        acc = conv_ref[DN_CONV - 1:DN_CONV, cols] * xbuf[top:top + n, cols]
        for s in range(1, DN_CONV):
            acc += conv_ref[DN_CONV - 1 - s:DN_CONV - s, cols] * xbuf[top - s:top - s + n, cols]
        return acc * jax.nn.sigmoid(acc)

    row = _iota(shape, 0)
    col = _iota(shape, 1)
    same = lax.shift_right_logical(row, 6) == lax.shift_right_logical(col, 6)
    incl = same & (row >= col)
    strict = same & (row > col)
    first_chunk_cols = col < DN_CHUNK
    eye = jnp.where(row == col, 1.0, 0.0).astype(BF16)
    tri = jnp.where(incl, 1.0, 0.0)
    last = jnp.where(col == jnp.where(row >= DN_CHUNK, n - 1, DN_CHUNK - 1), 1.0, 0.0)

    heads = range(DN_HEADS)
    c = DN_CHUNK

    def prep_units(r0):
        st = dict(lows=[], rhss=[], attns=[], q_decs=[], kd_t0s=[], kd_t1s=[], ends0=[], ends1=[])

        def head_unit(h):
            if h == 0:
                sm = sm_ref[r0:r0 + n, :].astype(F32)
                g_all = -jnp.exp(alog_ref[...]) * _softplus(sm + dtb_ref[...])
                st["beta_all"] = jax.nn.sigmoid(sm)
                st["gc_all"] = _dot(tri, g_all, HIGHEST)
                st["gl_all"] = _dot(last, st["gc_all"], HIGHEST)
                st["gc_t"] = st["gc_all"].T
            qa = conv_silu(r0, h * HEAD_DIM)
            ka = conv_silu(r0, DN_WIDTH + h * HEAD_DIM)
            va = conv_silu(r0, 2 * DN_WIDTH + h * HEAD_DIM)
            qn = qa * lax.rsqrt(jnp.sum(qa * qa, axis=-1, keepdims=True) + EPS) * HEAD_DIM ** -0.5
            kn = ka * lax.rsqrt(jnp.sum(ka * ka, axis=-1, keepdims=True) + EPS)
            beta = jnp.broadcast_to(st["beta_all"][:, DN_HEADS + h:DN_HEADS + h + 1], shape)
            g_col = jnp.broadcast_to(st["gc_all"][:, h:h + 1], shape)
            g_row = jnp.broadcast_to(st["gc_t"][h:h + 1, :], shape)
            g_end = jnp.broadcast_to(st["gl_all"][:, h:h + 1], shape)
            decay = jnp.where(incl, jnp.exp(jnp.where(incl, g_col - g_row, 0.0)), 0.0)
            e_col = jnp.exp(g_col)
            knb = kn.astype(BF16)
            st["lows"].append(jnp.where(strict, beta * _dot_nt(knb, knb) * decay, 0.0))
            st["attns"].append((_dot_nt(qn.astype(BF16), knb) * decay).astype(BF16))
            st["rhss"].append(jnp.concatenate([va * beta, kn * beta * e_col], axis=1))
            st["q_decs"].append((qn * e_col).astype(BF16))
            kd_t = _dot_nt(eye, (kn * jnp.exp(g_end - g_col)).astype(BF16))
            st["kd_t0s"].append(jnp.where(first_chunk_cols, kd_t, 0.0).astype(BF16))
            st["kd_t1s"].append(jnp.where(first_chunk_cols, 0.0, kd_t).astype(BF16))
            decay_end = jnp.exp(g_end)
            st["ends0"].append(decay_end[0:1, :])
            st["ends1"].append(decay_end[c:c + 1, :])

        return st, [functools.partial(head_unit, h) for h in heads]

    def solve_units(st):
        def first():
            st["ms"] = [(-low).astype(BF16) for low in st["lows"]]
            st["xs"] = [rhs + _dot(m, rhs.astype(BF16)) for m, rhs in zip(st["ms"], st["rhss"])]

        def double():
            st["ms"] = [_dot(m, m).astype(BF16) for m in st["ms"]]
            st["xs"] = [x + _dot(m, x.astype(BF16)) for m, x in zip(st["ms"], st["xs"])]

        def split():
            st["us"] = [x[:, :HEAD_DIM] for x in st["xs"]]
            st["ws"] = [x[:, HEAD_DIM:].astype(BF16) for x in st["xs"]]

        return [first] + [double] * 5 + [split]

    def scan_units(st, r0):
        def chunk0():
            st["s0"] = [s_ref[h] for h in heads]
            s0b = [s.astype(BF16) for s in st["s0"]]
            st["v0"] = [st["us"][h][:c] - _dot(st["ws"][h][:c], s0b[h]) for h in heads]
            st["o0"] = [_dot(st["q_decs"][h][:c], s0b[h]) for h in heads]

        def state1():
            vf0 = [jnp.concatenate([v, jnp.zeros_like(v)], axis=0).astype(BF16) for v in st["v0"]]
            st["s1"] = [st["s0"][h] * st["ends0"][h] + _dot(st["kd_t0s"][h], vf0[h]) for h in heads]

        def chunk1():
            s1b = [s.astype(BF16) for s in st["s1"]]
            st["v1"] = [st["us"][h][c:] - _dot(st["ws"][h][c:], s1b[h]) for h in heads]
            st["o1"] = [_dot(st["q_decs"][h][c:], s1b[h]) for h in heads]
            st["vf"] = [jnp.concatenate([st["v0"][h], st["v1"][h]], axis=0).astype(BF16)
                        for h in heads]

        def state2():
            for h in heads:
                s_ref[h] = st["s1"][h] * st["ends1"][h] + _dot(st["kd_t1s"][h], st["vf"][h])

        def output():
            for h in heads:
                hc = slice(h * HEAD_DIM, (h + 1) * HEAD_DIM)
                o = (jnp.concatenate([st["o0"][h], st["o1"][h]], axis=0)
                     + _dot(st["attns"][h], st["vf"][h]))
                z = z_ref[r0:r0 + n, hc].astype(F32)
                o_ref[r0:r0 + n, hc] = (
                    _rms(o, onorm_ref[...]) * (z * jax.nn.sigmoid(z))).astype(o_ref.dtype)

        return [chunk0, state1, chunk1, state2, output]

    def emit(*streams):
        streams = [list(s) for s in streams]
        while any(streams):
            for s in streams:
                if s:
                    s.pop(0)()

    prev, prep = prep_units(0)
    emit(prep)
    pending_scan = []
    for t in range(1, n_tiles):
        st, prep = prep_units(t * n)
        emit(solve_units(prev), prep, pending_scan)
        pending_scan = scan_units(prev, (t - 1) * n)
        prev = st
    emit(solve_units(prev), pending_scan)
    emit(scan_units(prev, (n_tiles - 1) * n))

    xbuf[0:DN_HALO, :] = xbuf[n_tiles * n:n_tiles * n + DN_HALO, :]


def _deltanet(p3, conv_w, a_log, dt_bias, out_norm):
    b, s, _ = p3.shape
    rows = DN_STEP_TILES * DN_TILE
    wide = lambda c: pl.BlockSpec((None, rows, DN_WIDTH), lambda i, t: (i, t, c // DN_WIDTH))
    lane_pad = lambda a: jnp.pad(a.reshape(1, -1), ((0, 0), (0, LANES - a.shape[-1])))
    full = lambda shp: pl.BlockSpec(shp, lambda i, t: (0,) * len(shp))
    return pl.pallas_call(
        _dn_body,
        out_shape=jax.ShapeDtypeStruct((b, s, DN_WIDTH), BF16),
        grid=(b, s // rows),
        in_specs=[wide(C_DQ), wide(C_DK), wide(C_DV), wide(C_DZ),
                  pl.BlockSpec((None, rows, LANES), lambda i, t: (i, t, C_SMALL // LANES)),
                  full((DN_CONV, 3 * DN_WIDTH)), full((1, LANES)), full((1, LANES)),
                  full((1, HEAD_DIM))],
        out_specs=pl.BlockSpec((None, rows, DN_WIDTH), lambda i, t: (i, t, 0)),
        scratch_shapes=[pltpu.VMEM((DN_HALO + rows, 3 * DN_WIDTH), F32),
                        pltpu.VMEM((DN_HEADS, HEAD_DIM, HEAD_DIM), F32)],
        compiler_params=_cparams(("parallel", "arbitrary")),
        name="deltanet",
    )(p3, p3, p3, p3, p3, conv_w, lane_pad(a_log), lane_pad(dt_bias), out_norm.reshape(1, -1))


def _pos_features(n, stride, offset):
    pos = _iota((n, LANES), 0) * stride + offset
    lane = _iota((n, LANES), 1)
    feat = jnp.where(lane < POS_SPLIT, lax.shift_right_logical(pos, POS_SHIFT),
                     jnp.where(lane < 2 * POS_SPLIT, pos & ((1 << POS_SHIFT) - 1), 0))
    return feat.astype(F32).astype(BF16)


def _nsa_keys_body(ks_ref, vs_ref, kw_ref, vw_ref, knorm_ref, ks_o, vs_o, kw_o, vw_o):
    rows = ks_o.shape[0]
    feat = _pos_features(rows, 1, pl.program_id(2) * rows)
    ks_o[:, :HEAD_DIM] = _rms(ks_ref[...].astype(F32), knorm_ref[1:2, :]).astype(BF16)
    ks_o[:, HEAD_DIM:] = feat
    vs_o[...] = vs_ref[...]
    kw_o[:, :HEAD_DIM] = _rms(kw_ref[...].astype(F32), knorm_ref[2:3, :]).astype(BF16)
    kw_o[:, HEAD_DIM:] = feat
    vw_o[...] = vw_ref[...]


def _nsa_cmp_body(xk_ref, xv_ref, kpos_ref, kw1_ref, kw2_ref, vpos_ref, vw1_ref, vw2_ref,
                  knorm_ref, kc_o, vc_o, xf_ref):
    n_sub = kc_o.shape[0]

    def compress(x_ref, pos_ref, w1_ref, w2_ref):
        xf_ref[...] = x_ref[...].astype(F32)
        a = jnp.zeros((n_sub, CMP_HIDDEN), F32)
        b = jnp.zeros((n_sub, CMP_HIDDEN), F32)
        for tau in range(CMP_STRIDE):
            x_tau = xf_ref[pl.ds(tau, n_sub, stride=CMP_STRIDE), :]
            lo, hi = tau, CMP_STRIDE + tau
            a += _dot((x_tau + pos_ref[lo:lo + 1, :]).astype(BF16), w1_ref[lo])
            b += _dot((x_tau + pos_ref[hi:hi + 1, :]).astype(BF16), w1_ref[hi])
        hid = a + pltpu.roll(b, n_sub - 1, axis=0)
        hid = hid * jax.nn.sigmoid(hid)
        return _dot(hid.astype(BF16), w2_ref[...])

    kc_o[:, :HEAD_DIM] = _rms(compress(xk_ref, kpos_ref, kw1_ref, kw2_ref),
                              knorm_ref[0:1, :]).astype(BF16)
    kc_o[:, HEAD_DIM:] = _pos_features(n_sub, CMP_STRIDE, CMP_BLOCK - 1)
    vc_o[...] = compress(xv_ref, vpos_ref, vw1_ref, vw2_ref).astype(BF16)


def _nsa_prep(p3, k_norm, cmpk_pos, cmpk_w1, cmpk_w2, cmpv_pos, cmpv_w1, cmpv_w2, *, rows=2048):
    b, s, _ = p3.shape
    g = NSA_GROUPS
    n_sub = s // CMP_STRIDE
    kv = g * HEAD_DIM
    k_aug = 2 * HEAD_DIM

    full2 = lambda shp: pl.BlockSpec(shp, lambda i, j: (0,) * len(shp))
    seq_in = lambda c0: pl.BlockSpec((None, s, HEAD_DIM), lambda i, j: (i, 0, c0 // HEAD_DIM + j))
    cmp_io = lambda w: pl.BlockSpec((None, None, n_sub, w), lambda i, j: (i, j, 0, 0))
    cmp_shape = lambda w: jax.ShapeDtypeStruct((b, g, n_sub, w), BF16)
    w1_shape = (CMP_BLOCK, HEAD_DIM, CMP_HIDDEN)
    kc, vc = pl.pallas_call(
        _nsa_cmp_body,
        out_shape=(cmp_shape(k_aug), cmp_shape(HEAD_DIM)),
        grid=(b, g),
        in_specs=[seq_in(C_NKV), seq_in(C_NKV + kv),
                  full2((CMP_BLOCK, HEAD_DIM)), full2(w1_shape), full2((CMP_HIDDEN, HEAD_DIM)),
                  full2((CMP_BLOCK, HEAD_DIM)), full2(w1_shape), full2((CMP_HIDDEN, HEAD_DIM)),
                  full2((3, HEAD_DIM))],
        out_specs=(cmp_io(k_aug), cmp_io(HEAD_DIM)),
        scratch_shapes=[pltpu.VMEM((s, HEAD_DIM), F32)],
        compiler_params=_cparams(("parallel", "parallel")),
        name="nsa_cmp",
    )(p3, p3, cmpk_pos, cmpk_w1.reshape(w1_shape).astype(BF16), cmpk_w2.astype(BF16),
      cmpv_pos, cmpv_w1.reshape(w1_shape).astype(BF16), cmpv_w2.astype(BF16), k_norm)

    rows = min(rows, s)
    col = lambda c0: pl.BlockSpec((None, rows, HEAD_DIM),
                                  lambda i, j, t: (i, t, c0 // HEAD_DIM + j))
    key_io = lambda w: pl.BlockSpec((None, None, rows, w), lambda i, j, t: (i, j, t, 0))
    key_shape = lambda w: jax.ShapeDtypeStruct((b, g, s, w), BF16)
    ks, vs, kw, vw = pl.pallas_call(
        _nsa_keys_body,
        out_shape=(key_shape(k_aug), key_shape(HEAD_DIM), key_shape(k_aug), key_shape(HEAD_DIM)),
        grid=(b, g, s // rows),
        in_specs=[col(C_NKV + 2 * kv), col(C_NKV + 3 * kv), col(C_NKV + 4 * kv), col(C_NKV + 5 * kv),
                  pl.BlockSpec((3, HEAD_DIM), lambda i, j, t: (0, 0))],
        out_specs=(key_io(k_aug), key_io(HEAD_DIM), key_io(k_aug), key_io(HEAD_DIM)),
        compiler_params=_cparams(("parallel", "parallel", "parallel")),
        name="nsa_keys",
    )(p3, p3, p3, p3, k_norm)
    return kc, vc, ks, vs, kw, vw


def _split3(x):
    hi = x.astype(BF16).astype(F32)
    mid = (x - hi).astype(BF16).astype(F32)
    return hi, mid, x - hi - mid


def _nsa_body(q_ref, sm_ref, kc_ref, vc_ref, ks_ref, vs_ref, kw_ref, vw_ref,
              qn_ref, ovt_ref, e_ref, tg_ref, o_ref,
              m_ref, l_ref, acc_ref, part_ref, flag_ref, *, top_n):
    grp = pl.program_id(1)
    qb = pl.program_id(2)
    t0 = qb * Q_TILE
    nq = Q_TILE
    nr = NSA_HPG * nq
    ncp = kc_ref.shape[0]
    hrows = [slice(h * nq, (h + 1) * nq) for h in range(NSA_HPG)]
    hcols = [slice(h * HEAD_DIM, (h + 1) * HEAD_DIM) for h in range(NSA_HPG)]

    head = (lax.shift_right_logical(_iota((nr, 1), 0), Q_SHIFT) + grp * NSA_HPG + 1).astype(F32)
    slope = jnp.exp(head * (-ALIBI_MAX / NSA_HEADS * np.log(2.0))) * LOG2E
    lane = _iota((nr, LANES), 1)
    slope_feat = jnp.zeros((nr, LANES), F32)
    for i, term in enumerate(_split3(slope)):
        slope_feat = jnp.where(lane == i, term * float(1 << POS_SHIFT), slope_feat)
        slope_feat = jnp.where(lane == POS_SPLIT + i, term, slope_feat)
    qh = jnp.concatenate(
        [_rms(q_ref[:, hcols[h]].astype(F32), qn_ref[...]) * (HEAD_DIM ** -0.5 * LOG2E)
         for h in range(NSA_HPG)], axis=0)
    q = jnp.concatenate([qh, slope_feat], axis=1).astype(BF16)

    rc = NSA_ROW_CHUNK
    row_chunks = [(h * nq + q0, q0) for q0 in range(0, nq, rc) for h in range(NSA_HPG)]

    def stacked(parts):
        return jnp.concatenate([parts[r0] for r0 in sorted(parts)], axis=0)

    def exp_chunk(s, bias, r0, q0):
        sc = s[r0:r0 + rc] + bias[q0:q0 + rc]
        e = jnp.exp2(sc - jnp.max(sc, axis=-1, keepdims=True))
        return e, jnp.sum(e, axis=-1, keepdims=True)

    gates = jax.nn.sigmoid(sm_ref[...].astype(F32))

    def gate(branch, h):
        at = lambda gv: 2 * DN_HEADS + branch * NSA_HEADS + gv * NSA_HPG + h
        cols = [gates[:, at(gv):at(gv) + 1] for gv in range(NSA_GROUPS)]
        return jnp.where(grp == 0, cols[0], cols[1])

    s_c = _dot_nt(q, kc_ref[...])
    mask_c = (t0 + _iota((nq, ncp), 0)) >= (_iota((nq, ncp), 1) * CMP_STRIDE + (CMP_BLOCK - 1))
    bias_c = jnp.where(mask_c, 0.0, NEG)
    sees_block = (t0 + _iota((nq, 1), 0)) >= CMP_BLOCK - 1
    e_c, inv_c, grp_c = {}, {}, {}
    for r0, q0 in row_chunks:
        e, total = exp_chunk(s_c, bias_c, r0, q0)
        inv_c[r0] = jnp.where(sees_block[q0:q0 + rc], 1.0 / total, 0.0)
        e_c[r0] = e.astype(BF16)
        grp_c[q0] = e * inv_c[r0] + (grp_c[q0] if q0 in grp_c else 0.0)
    o_c = _dot(stacked(e_c), vc_ref[...]) * stacked(inv_c)
    p_grp = stacked(grp_c)

    imp = _dot_nt(ovt_ref[...], p_grp, HIGHEST)
    blk = _iota((LANES, nq), 0)
    cur = lax.shift_right_logical(t0 + _iota((LANES, nq), 1), 6)
    forced = (blk == 0) | (blk == cur) | (blk == cur - 1)
    score = jnp.where(forced, -3e38, jnp.where(blk <= cur, imp, -1e6))
    sel_t = jnp.where(forced, 1.0, 0.0)
    blk_f = blk.astype(F32)

    wk = WINDOW + nq
    w0 = pl.multiple_of(jnp.maximum(t0 - WINDOW, 0), nq)
    s_w = _dot_nt(q, kw_ref[pl.ds(w0, wk), :])
    dist_w = (t0 + _iota((nq, wk), 0)) - (w0 + _iota((nq, wk), 1))
    bias_w = jnp.where((dist_w >= 0) & (dist_w < WINDOW), 0.0, NEG)
    e_w, inv_w = {}, {}
    rounds = top_n - 3
    for r in range(rounds):
        best = jnp.max(score, axis=0, keepdims=True)
        first = jnp.min(jnp.where(score == best, blk_f, float(LANES)), axis=0, keepdims=True)
        pick = blk_f == first
        sel_t = jnp.where(pick, 1.0, sel_t)
        score = jnp.where(pick, -3e38, score)
        for r0, q0 in row_chunks[len(e_w):(r + 1) * len(row_chunks) // rounds]:
            e, total = exp_chunk(s_w, bias_w, r0, q0)
            e_w[r0], inv_w[r0] = e.astype(BF16), 1.0 / total
    eye = jnp.where(_iota((nq, nq), 0) == _iota((nq, nq), 1), 1.0, 0.0).astype(BF16)
    sel = _dot_nt(eye, sel_t.astype(BF16)).astype(BF16)
    o_w = _dot(stacked(e_w), vw_ref[pl.ds(w0, wk), :]) * stacked(inv_w)
    for h in range(NSA_HPG):
        part_ref[:, hcols[h]] = gate(0, h) * o_c[hrows[h]] + gate(2, h) * o_w[hrows[h]]

    picked = jnp.broadcast_to(jnp.sum(sel.astype(F32), axis=0, keepdims=True), (8, LANES))
    tile_count = _dot(picked.astype(BF16), tg_ref[...])
    for t in range(flag_ref.shape[0]):
        flag_ref[t] = (tile_count[0, t] > 0.0).astype(jnp.int32)

    m_ref[...] = jnp.full(m_ref.shape, NEG, F32)
    l_ref[...] = jnp.zeros(l_ref.shape, F32)
    acc_ref[...] = jnp.zeros(acc_ref.shape, F32)
    kt_diag = lax.shift_right_logical(t0, 9)

    def sel_tile(kt, tiles, causal):
        width = tiles * SEL_KT
        k0 = pl.multiple_of(kt * SEL_KT, SEL_KT)
        s = _dot_nt(q, ks_ref[pl.ds(k0, width), :])
        mask = jnp.concatenate([_dot(sel, e_ref[kt + i]) for i in range(tiles)], axis=1) > 0.5
        if causal:
            mask = mask & ((t0 + _iota((nq, width), 0)) >= (k0 + _iota((nq, width), 1)))
        bias = jnp.where(mask, 0.0, NEG)
        es, alphas = {}, {}
        for r0, q0 in row_chunks:
            rows = slice(r0, r0 + rc)
            sc = s[rows] + bias[q0:q0 + rc]
            m_prev = m_ref[rows, :]
            m_new = jnp.maximum(m_prev, jnp.max(sc, axis=-1, keepdims=True))
            alphas[r0] = jnp.exp2(m_prev - m_new)
            e = jnp.exp2(sc - m_new[:, 0:1])
            l_ref[rows, :] = alphas[r0] * l_ref[rows, :] + jnp.sum(e, axis=-1, keepdims=True)
            m_ref[rows, :] = m_new
            es[r0] = e.astype(BF16)
        pv = _dot(stacked(es), vs_ref[pl.ds(k0, width), :])
        acc_ref[...] = stacked(alphas) * acc_ref[...] + pv

    kt_first = jnp.maximum(kt_diag - 1, 0)
    sel_tile(kt_first, 2, True)

    def earlier_tile(i, carry):
        kt = kt_first - 1 - i

        @pl.when(flag_ref[kt] > 0)
        def _():
            sel_tile(kt, 1, False)

        return carry

    lax.fori_loop(0, kt_first, earlier_tile, 0)
    o_s = acc_ref[...] / l_ref[...]
    gates = jax.nn.sigmoid(sm_ref[...].astype(F32))
    for h in range(NSA_HPG):
        o = part_ref[:, hcols[h]] + gate(1, h) * o_s[hrows[h]]
        o_ref[:, hcols[h]] = o.astype(o_ref.dtype)


def _nsa(p3, kc, vc, ks, vs, kw, vw, q_norm):
    b, s, _ = p3.shape
    g = NSA_GROUPS
    ncp = s // CMP_STRIDE
    n_sel = s // SEL_BLOCK
    assert n_sel <= LANES and s % SEL_KT == 0 and s >= WINDOW + Q_TILE and NSA_GROUPS == 2
    assert SEL_KT % Q_TILE == 0 and WINDOW % Q_TILE == 0 and s >= 2 * SEL_KT
    assert min(SEL_TOPN, n_sel) - 3 >= NSA_HPG
    gw = NSA_HPG * HEAD_DIM
    ci = np.arange(ncp)[:, None]
    sj = np.arange(LANES)[None, :]
    per = SEL_BLOCK // CMP_STRIDE
    overlap = ((ci // per == sj).astype(np.float32) + ((ci + 1) // per == sj).astype(np.float32))
    overlap[ncp - 1:] = 0.0
    key_blk = (np.arange(s) // SEL_BLOCK).reshape(s // SEL_KT, 1, SEL_KT)
    expand = (key_blk == np.arange(LANES)[None, :, None]).astype(np.float32)

    n_tiles = s // SEL_KT
    tile_of = (np.arange(LANES)[:, None] // (SEL_KT // SEL_BLOCK)
               == np.arange(LANES)[None, :]).astype(np.float32)

    kv = lambda rows, w: pl.BlockSpec((None, None, rows, w), lambda i, j, t: (i, j, 0, 0))
    full = lambda shp: pl.BlockSpec(shp, lambda i, j, t: (0,) * len(shp))
    return pl.pallas_call(
        functools.partial(_nsa_body, top_n=min(SEL_TOPN, n_sel)),
        out_shape=jax.ShapeDtypeStruct((b, s, NSA_WIDTH), BF16),
        grid=(b, g, s // Q_TILE),
        in_specs=[pl.BlockSpec((None, Q_TILE, gw), lambda i, j, t: (i, t, C_NQ // gw + j)),
                  pl.BlockSpec((None, Q_TILE, LANES), lambda i, j, t: (i, t, C_SMALL // LANES)),
                  kv(ncp, 2 * HEAD_DIM), kv(ncp, HEAD_DIM), kv(s, 2 * HEAD_DIM), kv(s, HEAD_DIM),
                  kv(s, 2 * HEAD_DIM), kv(s, HEAD_DIM),
                  full((1, HEAD_DIM)), full((LANES, ncp)), full((n_tiles, LANES, SEL_KT)),
                  full((LANES, LANES))],
        out_specs=pl.BlockSpec((None, Q_TILE, gw), lambda i, j, t: (i, t, j)),
        scratch_shapes=[pltpu.VMEM((NSA_HPG * Q_TILE, HEAD_DIM), F32)] * 3
        + [pltpu.VMEM((Q_TILE, gw), F32), pltpu.SMEM((n_tiles,), jnp.int32)],
        compiler_params=_cparams(("parallel", "parallel", "arbitrary")),
        name="nsa",
    )(p3, p3, kc, vc, ks, vs, kw, vw, q_norm.reshape(1, -1),
      jnp.asarray(overlap.T), jnp.asarray(expand, dtype=BF16), jnp.asarray(tile_of, dtype=BF16))


def kernel(x, ffn1_norm, ffn1_gate, ffn1_up, ffn1_down, mix_norm, w_in, dn_conv, dn_a_log,
           dn_dt_bias, dn_out_norm, nsa_q_norm, nsa_k_norm, cmpk_pos, cmpk_w1, cmpk_w2,
           cmpv_pos, cmpv_w1, cmpv_w2, sgu_norm_w, sgu_norm_b, sgu_w, sgu_b, w_branch_a,
           w_branch_b, w_branch_c, w_out, ffn2_norm, ffn2_gate, ffn2_up, ffn2_down):
    batch, seq, d = x.shape
    t = batch * seq
    ffn1 = _ffn_weights(ffn1_gate, ffn1_up, ffn1_down)
    ffn2 = _ffn_weights(ffn2_gate, ffn2_up, ffn2_down)
    w_in = _reorder_w_in(w_in)
    mix = tuple(w.astype(BF16) for w in (w_branch_a, w_branch_b, w_branch_c, w_out))

    h = x.reshape(t, d)
    for l in range(ffn1_norm.shape[0]):
        h = _ffn(h, ffn1_norm[l], *ffn1, l)
        p = _inproj(h, mix_norm[l], w_in, l)
        p3 = p.reshape(batch, seq, P_COLS)
        y_a = _deltanet(p3, dn_conv[l], dn_a_log[l], dn_dt_bias[l], dn_out_norm[l])
        kv = _nsa_prep(p3, nsa_k_norm[l], cmpk_pos[l], cmpk_w1[l], cmpk_w2[l],
                       cmpv_pos[l], cmpv_w1[l], cmpv_w2[l])
        y_b = _nsa(p3, *kv, nsa_q_norm[l])
        y_c = _sgu(p, sgu_norm_w[l], sgu_norm_b[l], sgu_w[l], sgu_b[l])
        h = _merge(h, p, y_a.reshape(t, -1), y_b.reshape(t, -1), y_c, *mix, l)
        h = _ffn(h, ffn2_norm[l], *ffn2, l)
    return h.reshape(batch, seq, d)
```

```python
import functools

import jax
import jax.numpy as jnp
import numpy as np
from jax import lax
from jax.experimental import pallas as pl
from jax.experimental.pallas import tpu as pltpu

F32 = jnp.float32
BF16 = jnp.bfloat16
HIGHEST = lax.Precision.HIGHEST

LANES = 128
V7X_VMEM_LIMIT = 56 * 1024 * 1024

HEAD_DIM = 128
EPS = 1e-6
NORM_ROWS = 256
NEG = -1e30

DN_HEADS = 8
DN_WIDTH = DN_HEADS * HEAD_DIM
DN_CONV = 4
DN_CHUNK = 64
DN_TILE = 2 * DN_CHUNK
DN_HALO = 8
DN_STEP_TILES = 4

NSA_HEADS = 8
NSA_GROUPS = 2
NSA_HPG = NSA_HEADS // NSA_GROUPS
NSA_WIDTH = NSA_HEADS * HEAD_DIM
CMP_STRIDE = 16
CMP_BLOCK = 2 * CMP_STRIDE
CMP_HIDDEN = 256
SEL_BLOCK = 64
SEL_TOPN = 16
WINDOW = 512
Q_SHIFT = 8
Q_TILE = 1 << Q_SHIFT
ALIBI_MAX = 8.0
SEL_KT = 512
NSA_ROW_CHUNK = Q_TILE
POS_SHIFT = 6
POS_SPLIT = 3
LOG2E = 1.4426950408889634

SGU_GROUPS = 8
SGU_WIDTH = SGU_GROUPS * HEAD_DIM
SGU_CHUNK = 128


def _cparams(sem):
    return pltpu.CompilerParams(dimension_semantics=sem, vmem_limit_bytes=V7X_VMEM_LIMIT)


def _rms(x, w):
    return x * lax.rsqrt(jnp.mean(x * x, axis=-1, keepdims=True) + EPS) * w


def _iota(shape, dim):
    return lax.broadcasted_iota(jnp.int32, shape, dim)


def _dot(a, b, precision=None):
    return jnp.dot(a, b, preferred_element_type=F32, precision=precision)


def _dot_nt(a, b, precision=None):
    return lax.dot_general(a, b, (((1,), (1,)), ((), ())), preferred_element_type=F32,
                           precision=precision)


def _ffn_body(x_ref, nw_ref, wg_ref, wu_ref, wd_ref, o_ref, h_ref):
    def half_swiglu(h):
        g = _dot(h, wg_ref[...])
        u = _dot(h, wu_ref[...])
        return _dot((0.5 * g * jax.nn.sigmoid(g) * u).astype(BF16), wd_ref[...])

    @pl.when(pl.program_id(1) == 0)
    def _():
        for r0 in range(0, x_ref.shape[0], NORM_ROWS):
            rows = slice(r0, r0 + NORM_ROWS)
            x = x_ref[rows, :]
            h = _rms(x, nw_ref[...]).astype(BF16)
            h_ref[rows, :] = h
            o_ref[rows, :] = x + half_swiglu(h)

    @pl.when(pl.program_id(1) > 0)
    def _():
        o_ref[...] += half_swiglu(h_ref[...])


FF_TILE = 512


def _pad_cols_body(x_ref, o_ref):
    n = x_ref.shape[1]
    o_ref[:, :n] = x_ref[...].astype(BF16)
    o_ref[:, n:] = jnp.zeros((o_ref.shape[0], o_ref.shape[1] - n), BF16)


def _pad_rows_body(x_ref, o_ref, *, n_blocks):
    inside = pl.program_id(1) < n_blocks
    o_ref[...] = jnp.where(inside, x_ref[...], 0.0).astype(BF16)


def _cast_pad(w, axis, extra, *, rows=128):
    nl, r, c = w.shape
    if axis == 2:
        return pl.pallas_call(
            _pad_cols_body,
            out_shape=jax.ShapeDtypeStruct((nl, r, c + extra), BF16),
            grid=(nl, r // rows),
            in_specs=[pl.BlockSpec((None, rows, c), lambda l, i: (l, i, 0))],
            out_specs=pl.BlockSpec((None, rows, c + extra), lambda l, i: (l, i, 0)),
            compiler_params=_cparams(("parallel", "parallel")),
            name="cast_pad_cols",
        )(w)
    assert r % rows == 0 and extra % rows == 0
    n_blocks = r // rows
    return pl.pallas_call(
        functools.partial(_pad_rows_body, n_blocks=n_blocks),
        out_shape=jax.ShapeDtypeStruct((nl, r + extra, c), BF16),
        grid=(nl, (r + extra) // rows),
        in_specs=[pl.BlockSpec((None, rows, c), lambda l, i: (l, jnp.minimum(i, n_blocks - 1), 0))],
        out_specs=pl.BlockSpec((None, rows, c), lambda l, i: (l, i, 0)),
        compiler_params=_cparams(("parallel", "parallel")),
        name="cast_pad_rows",
    )(w)


def _ffn_weights(w_gate, w_up, w_down):
    ff = w_gate.shape[-1]
    extra = FF_TILE * pl.cdiv(ff, FF_TILE) - ff
    return _cast_pad(w_gate, 2, extra), _cast_pad(w_up, 2, extra), _cast_pad(w_down, 1, extra)


def _ffn(x, norm_w, wg, wu, wd, layer, *, tm=1024, tf=FF_TILE):
    t, d = x.shape
    ffp = wg.shape[-1]
    return pl.pallas_call(
        _ffn_body,
        out_shape=jax.ShapeDtypeStruct((t, d), F32),
        grid=(t // tm, ffp // tf),
        in_specs=[
            pl.BlockSpec((tm, d), lambda i, j: (i, 0)),
            pl.BlockSpec((1, d), lambda i, j: (0, 0)),
            pl.BlockSpec((None, d, tf), lambda i, j: (layer, 0, j)),
            pl.BlockSpec((None, d, tf), lambda i, j: (layer, 0, j)),
            pl.BlockSpec((None, tf, d), lambda i, j: (layer, j, 0)),
        ],
        out_specs=pl.BlockSpec((tm, d), lambda i, j: (i, 0)),
        scratch_shapes=[pltpu.VMEM((tm, d), BF16)],
        compiler_params=_cparams(("parallel", "arbitrary")),
        name="ffn",
    )(x, norm_w.reshape(1, d), wg, wu, wd)


C_DQ, C_DK, C_DV, C_DZ = 0, 1024, 2048, 3072
C_NQ = 4096
C_SU, C_SV = 5120, 6144
C_GA, C_GB, C_GC = 7168, 9216, 11264
C_NKV = 13312
C_SMALL = 14848
P_COLS = 15360


W_IN_GROUPS = ((0, 4096),
               (4112, 1024),
               (6696, 2048),
               (8744, 6144),
               (5136, 1536))
W_IN_SMALL = ((4096, 16),
              (6672, 24))


def _reorder_body(x_ref, o_ref):
    rows = o_ref.shape[0]
    dst = 0
    for src, n in W_IN_GROUPS:
        o_ref[:, dst:dst + n] = x_ref[:, src:src + n].astype(BF16)
        dst += n
    small = [x_ref[:, src:src + n] for src, n in W_IN_SMALL]
    small.append(jnp.zeros((rows, LANES - sum(n for _, n in W_IN_SMALL)), F32))
    o_ref[:, dst:dst + LANES] = jnp.concatenate(small, axis=1).astype(BF16)
    o_ref[:, dst + LANES:] = jnp.zeros((rows, o_ref.shape[1] - dst - LANES), BF16)


def _reorder_w_in(w_in, *, rows=128):
    nl, d, n = w_in.shape
    assert sum(w for _, w in W_IN_GROUPS + W_IN_SMALL) == n
    return pl.pallas_call(
        _reorder_body,
        out_shape=jax.ShapeDtypeStruct((nl, d, P_COLS), BF16),
        grid=(nl, d // rows),
        in_specs=[pl.BlockSpec((None, rows, n), lambda l, i: (l, i, 0))],
        out_specs=pl.BlockSpec((None, rows, P_COLS), lambda l, i: (l, i, 0)),
        compiler_params=_cparams(("parallel", "parallel")),
        name="reorder_w_in",
    )(w_in)


def _inproj_body(x_ref, nw_ref, w_ref, o_ref, h_ref):
    @pl.when(pl.program_id(1) == 0)
    def _():
        for r0 in range(0, x_ref.shape[0], NORM_ROWS):
            rows = slice(r0, r0 + NORM_ROWS)
            h = _rms(x_ref[rows, :], nw_ref[...]).astype(BF16)
            h_ref[rows, :] = h
            o_ref[rows, :] = _dot(h, w_ref[...]).astype(o_ref.dtype)

    @pl.when(pl.program_id(1) > 0)
    def _():
        o_ref[...] = _dot(h_ref[...], w_ref[...]).astype(o_ref.dtype)


def _inproj(x, norm_w, w, layer, *, tm=1024, tn=1536):
    t, d = x.shape
    n = w.shape[-1]
    return pl.pallas_call(
        _inproj_body,
        out_shape=jax.ShapeDtypeStruct((t, n), BF16),
        grid=(t // tm, n // tn),
        in_specs=[
            pl.BlockSpec((tm, d), lambda i, j: (i, 0)),
            pl.BlockSpec((1, d), lambda i, j: (0, 0)),
            pl.BlockSpec((None, d, tn), lambda i, j: (layer, 0, j)),
        ],
        out_specs=pl.BlockSpec((tm, tn), lambda i, j: (i, j)),
        scratch_shapes=[pltpu.VMEM((tm, d), BF16)],
        compiler_params=_cparams(("parallel", "arbitrary")),
        name="inproj",
    )(x, norm_w.reshape(1, d), w)


def _merge_body(x_ref, ya_ref, yb_ref, yc_ref, ga_ref, gb_ref, gc_ref,
                wa_ref, wb_ref, wc_ref, wo_ref, o_ref):
    @pl.when(pl.program_id(1) == 0)
    def _():
        o_ref[...] = x_ref[...]

    m = jax.nn.sigmoid(ga_ref[...].astype(F32)) * _dot(ya_ref[...], wa_ref[...])
    m += jax.nn.sigmoid(gb_ref[...].astype(F32)) * _dot(yb_ref[...], wb_ref[...])
    m += jax.nn.sigmoid(gc_ref[...].astype(F32)) * _dot(yc_ref[...], wc_ref[...])
    o_ref[...] += _dot(m.astype(BF16), wo_ref[...])


def _merge(x, p, ya, yb, yc, wa, wb, wc, wo, layer, *, tm=512, tn=1024):
    t, d = x.shape
    k = ya.shape[1]
    y_spec = pl.BlockSpec((tm, k), lambda i, j: (i, 0))
    w_spec = pl.BlockSpec((None, k, tn), lambda i, j: (layer, 0, j))

    def gate_spec(col):
        assert col % tn == 0
        return pl.BlockSpec((tm, tn), lambda i, j: (i, col // tn + j))

    return pl.pallas_call(
        _merge_body,
        out_shape=jax.ShapeDtypeStruct((t, d), F32),
        grid=(t // tm, d // tn),
        in_specs=[pl.BlockSpec((tm, d), lambda i, j: (i, 0)), y_spec, y_spec, y_spec,
                  gate_spec(C_GA), gate_spec(C_GB), gate_spec(C_GC),
                  w_spec, w_spec, w_spec,
                  pl.BlockSpec((None, tn, d), lambda i, j: (layer, j, 0))],
        out_specs=pl.BlockSpec((tm, d), lambda i, j: (i, 0)),
        compiler_params=_cparams(("parallel", "arbitrary")),
        name="merge",
    )(x, ya, yb, yc, p, p, p, wa, wb, wc, wo)


def _sgu_body(u_ref, v_ref, nw_ref, nb_ref, ws_ref, bs_ref, o_ref, *, n_chunk):
    v = jax.nn.gelu(v_ref[...].astype(F32))
    mu = jnp.mean(v, axis=-1, keepdims=True)
    vc = v - mu
    var = jnp.mean(vc * vc, axis=-1, keepdims=True)
    vn = (vc * lax.rsqrt(var + EPS) * nw_ref[...] + nb_ref[...]).astype(BF16)
    tril = _iota((SGU_CHUNK, SGU_CHUNK), 0) >= _iota((SGU_CHUNK, SGU_CHUNK), 1)
    for g in range(SGU_GROUPS):
        cols = slice(g * HEAD_DIM, (g + 1) * HEAD_DIM)
        w = jnp.where(tril, ws_ref[g], 0.0).astype(BF16)
        rhs = jnp.concatenate(
            [vn[c * SGU_CHUNK:(c + 1) * SGU_CHUNK, cols] for c in range(n_chunk)], axis=1)
        s = _dot(w, rhs)
        for c in range(n_chunk):
            rows = slice(c * SGU_CHUNK, (c + 1) * SGU_CHUNK)
            u = jax.nn.gelu(u_ref[rows, cols].astype(F32))
            sc = s[:, c * HEAD_DIM:(c + 1) * HEAD_DIM] + bs_ref[g]
            o_ref[rows, cols] = (u * sc).astype(o_ref.dtype)


def _sgu(p, norm_w, norm_b, w_s, b_s, *, n_chunk=4):
    t = p.shape[0]
    ts = n_chunk * SGU_CHUNK
    bias = jnp.broadcast_to(b_s[:, :, None], (SGU_GROUPS, SGU_CHUNK, HEAD_DIM))
    return pl.pallas_call(
        functools.partial(_sgu_body, n_chunk=n_chunk),
        out_shape=jax.ShapeDtypeStruct((t, SGU_WIDTH), BF16),
        grid=(t // ts,),
        in_specs=[
            pl.BlockSpec((ts, SGU_WIDTH), lambda i: (i, C_SU // SGU_WIDTH)),
            pl.BlockSpec((ts, SGU_WIDTH), lambda i: (i, C_SV // SGU_WIDTH)),
            pl.BlockSpec((1, SGU_WIDTH), lambda i: (0, 0)),
            pl.BlockSpec((1, SGU_WIDTH), lambda i: (0, 0)),
            pl.BlockSpec((SGU_GROUPS, SGU_CHUNK, SGU_CHUNK), lambda i: (0, 0, 0)),
            pl.BlockSpec((SGU_GROUPS, SGU_CHUNK, HEAD_DIM), lambda i: (0, 0, 0)),
        ],
        out_specs=pl.BlockSpec((ts, SGU_WIDTH), lambda i: (i, 0)),
        compiler_params=_cparams(("parallel",)),
        name="sgu",
    )(p, p, norm_w.reshape(1, -1), norm_b.reshape(1, -1), w_s, bias)


def _softplus(x):
    return jnp.maximum(x, 0.0) + jnp.log(1.0 + jnp.exp(-jnp.abs(x)))


def _dn_body(q_ref, k_ref, v_ref, z_ref, sm_ref, conv_ref, alog_ref, dtb_ref, onorm_ref,
             o_ref, xbuf, s_ref):
    n = DN_TILE
    shape = (n, n)
    n_tiles = q_ref.shape[0] // n

    @pl.when(pl.program_id(1) == 0)
    def _():
        xbuf[0:DN_HALO, :] = jnp.zeros((DN_HALO, 3 * DN_WIDTH), F32)
        s_ref[...] = jnp.zeros_like(s_ref)

    xbuf[DN_HALO:, 0:DN_WIDTH] = q_ref[...].astype(F32)
    xbuf[DN_HALO:, DN_WIDTH:2 * DN_WIDTH] = k_ref[...].astype(F32)
    xbuf[DN_HALO:, 2 * DN_WIDTH:] = v_ref[...].astype(F32)

    def conv_silu(r0, c0):
        cols = slice(c0, c0 + HEAD_DIM)
        top = DN_HALO + r0
        acc = conv_ref[DN_CONV - 1:DN_CONV, cols] * xbuf[top:top + n, cols]
        for s in range(1, DN_CONV):
            acc += conv_ref[DN_CONV - 1 - s:DN_CONV - s, cols] * xbuf[top - s:top - s + n, cols]
        return acc * jax.nn.sigmoid(acc)

    row = _iota(shape, 0)
    col = _iota(shape, 1)
    same = lax.shift_right_logical(row, 6) == lax.shift_right_logical(col, 6)
    incl = same & (row >= col)
    strict = same & (row > col)
    first_chunk_cols = col < DN_CHUNK
    eye = jnp.where(row == col, 1.0, 0.0).astype(BF16)
    tri = jnp.where(incl, 1.0, 0.0)
    last = jnp.where(col == jnp.where(row >= DN_CHUNK, n - 1, DN_CHUNK - 1), 1.0, 0.0)

    heads = range(DN_HEADS)
    c = DN_CHUNK

    def prep_units(r0):
        st = dict(lows=[], rhss=[], attns=[], q_decs=[], kd_t0s=[], kd_t1s=[], ends0=[], ends1=[])

        def head_unit(h):
            if h == 0:
                sm = sm_ref[r0:r0 + n, :].astype(F32)
                g_all = -jnp.exp(alog_ref[...]) * _softplus(sm + dtb_ref[...])
                st["beta_all"] = jax.nn.sigmoid(sm)
                st["gc_all"] = _dot(tri, g_all, HIGHEST)
                st["gl_all"] = _dot(last, st["gc_all"], HIGHEST)
                st["gc_t"] = st["gc_all"].T
            qa = conv_silu(r0, h * HEAD_DIM)
            ka = conv_silu(r0, DN_WIDTH + h * HEAD_DIM)
            va = conv_silu(r0, 2 * DN_WIDTH + h * HEAD_DIM)
            qn = qa * lax.rsqrt(jnp.sum(qa * qa, axis=-1, keepdims=True) + EPS) * HEAD_DIM ** -0.5
            kn = ka * lax.rsqrt(jnp.sum(ka * ka, axis=-1, keepdims=True) + EPS)
            beta = jnp.broadcast_to(st["beta_all"][:, DN_HEADS + h:DN_HEADS + h + 1], shape)
            g_col = jnp.broadcast_to(st["gc_all"][:, h:h + 1], shape)
            g_row = jnp.broadcast_to(st["gc_t"][h:h + 1, :], shape)
            g_end = jnp.broadcast_to(st["gl_all"][:, h:h + 1], shape)
            decay = jnp.where(incl, jnp.exp(jnp.where(incl, g_col - g_row, 0.0)), 0.0)
            e_col = jnp.exp(g_col)
            knb = kn.astype(BF16)
            st["lows"].append(jnp.where(strict, beta * _dot_nt(knb, knb) * decay, 0.0))
            st["attns"].append((_dot_nt(qn.astype(BF16), knb) * decay).astype(BF16))
            st["rhss"].append(jnp.concatenate([va * beta, kn * beta * e_col], axis=1))
            st["q_decs"].append((qn * e_col).astype(BF16))
            kd_t = _dot_nt(eye, (kn * jnp.exp(g_end - g_col)).astype(BF16))
            st["kd_t0s"].append(jnp.where(first_chunk_cols, kd_t, 0.0).astype(BF16))
            st["kd_t1s"].append(jnp.where(first_chunk_cols, 0.0, kd_t).astype(BF16))
            decay_end = jnp.exp(g_end)
            st["ends0"].append(decay_end[0:1, :])
            st["ends1"].append(decay_end[c:c + 1, :])

        return st, [functools.partial(head_unit, h) for h in heads]

    def solve_units(st):
        def first():
            st["ms"] = [(-low).astype(BF16) for low in st["lows"]]
            st["xs"] = [rhs + _dot(m, rhs.astype(BF16)) for m, rhs in zip(st["ms"], st["rhss"])]

        def double():
            st["ms"] = [_dot(m, m).astype(BF16) for m in st["ms"]]
            st["xs"] = [x + _dot(m, x.astype(BF16)) for m, x in zip(st["ms"], st["xs"])]

        def split():
            st["us"] = [x[:, :HEAD_DIM] for x in st["xs"]]
            st["ws"] = [x[:, HEAD_DIM:].astype(BF16) for x in st["xs"]]

        return [first] + [double] * 5 + [split]

    def scan_units(st, r0):
        def chunk0():
            st["s0"] = [s_ref[h] for h in heads]
            s0b = [s.astype(BF16) for s in st["s0"]]
            st["v0"] = [st["us"][h][:c] - _dot(st["ws"][h][:c], s0b[h]) for h in heads]
            st["o0"] = [_dot(st["q_decs"][h][:c], s0b[h]) for h in heads]

        def state1():
            vf0 = [jnp.concatenate([v, jnp.zeros_like(v)], axis=0).astype(BF16) for v in st["v0"]]
            st["s1"] = [st["s0"][h] * st["ends0"][h] + _dot(st["kd_t0s"][h], vf0[h]) for h in heads]

        def chunk1():
            s1b = [s.astype(BF16) for s in st["s1"]]
            st["v1"] = [st["us"][h][c:] - _dot(st["ws"][h][c:], s1b[h]) for h in heads]
            st["o1"] = [_dot(st["q_decs"][h][c:], s1b[h]) for h in heads]
            st["vf"] = [jnp.concatenate([st["v0"][h], st["v1"][h]], axis=0).astype(BF16)
                        for h in heads]

        def state2():
            for h in heads:
                s_ref[h] = st["s1"][h] * st["ends1"][h] + _dot(st["kd_t1s"][h], st["vf"][h])

        def output():
            for h in heads:
                hc = slice(h * HEAD_DIM, (h + 1) * HEAD_DIM)
                o = (jnp.concatenate([st["o0"][h], st["o1"][h]], axis=0)
                     + _dot(st["attns"][h], st["vf"][h]))
                z = z_ref[r0:r0 + n, hc].astype(F32)
                o_ref[r0:r0 + n, hc] = (
                    _rms(o, onorm_ref[...]) * (z * jax.nn.sigmoid(z))).astype(o_ref.dtype)

        return [chunk0, state1, chunk1, state2, output]

    def emit(*streams):
        streams = [list(s) for s in streams]
        while any(streams):
            for s in streams:
                if s:
                    s.pop(0)()

    prev, prep = prep_units(0)
    emit(prep)
    pending_scan = []
    for t in range(1, n_tiles):
        st, prep = prep_units(t * n)
        emit(solve_units(prev), prep, pending_scan)
        pending_scan = scan_units(prev, (t - 1) * n)
        prev = st
    emit(solve_units(prev), pending_scan)
    emit(scan_units(prev, (n_tiles - 1) * n))

    xbuf[0:DN_HALO, :] = xbuf[n_tiles * n:n_tiles * n + DN_HALO, :]


def _deltanet(p3, conv_w, a_log, dt_bias, out_norm):
    b, s, _ = p3.shape
    rows = DN_STEP_TILES * DN_TILE
    wide = lambda c: pl.BlockSpec((None, rows, DN_WIDTH), lambda i, t: (i, t, c // DN_WIDTH))
    lane_pad = lambda a: jnp.pad(a.reshape(1, -1), ((0, 0), (0, LANES - a.shape[-1])))
    full = lambda shp: pl.BlockSpec(shp, lambda i, t: (0,) * len(shp))
    return pl.pallas_call(
        _dn_body,
        out_shape=jax.ShapeDtypeStruct((b, s, DN_WIDTH), BF16),
        grid=(b, s // rows),
        in_specs=[wide(C_DQ), wide(C_DK), wide(C_DV), wide(C_DZ),
                  pl.BlockSpec((None, rows, LANES), lambda i, t: (i, t, C_SMALL // LANES)),
                  full((DN_CONV, 3 * DN_WIDTH)), full((1, LANES)), full((1, LANES)),
                  full((1, HEAD_DIM))],
        out_specs=pl.BlockSpec((None, rows, DN_WIDTH), lambda i, t: (i, t, 0)),
        scratch_shapes=[pltpu.VMEM((DN_HALO + rows, 3 * DN_WIDTH), F32),
                        pltpu.VMEM((DN_HEADS, HEAD_DIM, HEAD_DIM), F32)],
        compiler_params=_cparams(("parallel", "arbitrary")),
        name="deltanet",
    )(p3, p3, p3, p3, p3, conv_w, lane_pad(a_log), lane_pad(dt_bias), out_norm.reshape(1, -1))


def _pos_features(n, stride, offset):
    pos = _iota((n, LANES), 0) * stride + offset
    lane = _iota((n, LANES), 1)
    feat = jnp.where(lane < POS_SPLIT, lax.shift_right_logical(pos, POS_SHIFT),
                     jnp.where(lane < 2 * POS_SPLIT, pos & ((1 << POS_SHIFT) - 1), 0))
    return feat.astype(F32).astype(BF16)


def _nsa_keys_body(ks_ref, vs_ref, kw_ref, vw_ref, knorm_ref, ks_o, vs_o, kw_o, vw_o):
    rows = ks_o.shape[0]
    feat = _pos_features(rows, 1, pl.program_id(2) * rows)
    ks_o[:, :HEAD_DIM] = _rms(ks_ref[...].astype(F32), knorm_ref[1:2, :]).astype(BF16)
    ks_o[:, HEAD_DIM:] = feat
    vs_o[...] = vs_ref[...]
    kw_o[:, :HEAD_DIM] = _rms(kw_ref[...].astype(F32), knorm_ref[2:3, :]).astype(BF16)
    kw_o[:, HEAD_DIM:] = feat
    vw_o[...] = vw_ref[...]


def _nsa_cmp_body(xk_ref, xv_ref, kpos_ref, kw1_ref, kw2_ref, vpos_ref, vw1_ref, vw2_ref,
                  knorm_ref, kc_o, vc_o, xf_ref):
    n_sub = kc_o.shape[0]

    def compress(x_ref, pos_ref, w1_ref, w2_ref):
        xf_ref[...] = x_ref[...].astype(F32)
        a = jnp.zeros((n_sub, CMP_HIDDEN), F32)
        b = jnp.zeros((n_sub, CMP_HIDDEN), F32)
        for tau in range(CMP_STRIDE):
            x_tau = xf_ref[pl.ds(tau, n_sub, stride=CMP_STRIDE), :]
            lo, hi = tau, CMP_STRIDE + tau
            a += _dot((x_tau + pos_ref[lo:lo + 1, :]).astype(BF16), w1_ref[lo])
            b += _dot((x_tau + pos_ref[hi:hi + 1, :]).astype(BF16), w1_ref[hi])
        hid = a + pltpu.roll(b, n_sub - 1, axis=0)
        hid = hid * jax.nn.sigmoid(hid)
        return _dot(hid.astype(BF16), w2_ref[...])

    kc_o[:, :HEAD_DIM] = _rms(compress(xk_ref, kpos_ref, kw1_ref, kw2_ref),
                              knorm_ref[0:1, :]).astype(BF16)
    kc_o[:, HEAD_DIM:] = _pos_features(n_sub, CMP_STRIDE, CMP_BLOCK - 1)
    vc_o[...] = compress(xv_ref, vpos_ref, vw1_ref, vw2_ref).astype(BF16)


def _nsa_prep(p3, k_norm, cmpk_pos, cmpk_w1, cmpk_w2, cmpv_pos, cmpv_w1, cmpv_w2, *, rows=2048):
    b, s, _ = p3.shape
    g = NSA_GROUPS
    n_sub = s // CMP_STRIDE
    kv = g * HEAD_DIM
    k_aug = 2 * HEAD_DIM

    full2 = lambda shp: pl.BlockSpec(shp, lambda i, j: (0,) * len(shp))
    seq_in = lambda c0: pl.BlockSpec((None, s, HEAD_DIM), lambda i, j: (i, 0, c0 // HEAD_DIM + j))
    cmp_io = lambda w: pl.BlockSpec((None, None, n_sub, w), lambda i, j: (i, j, 0, 0))
    cmp_shape = lambda w: jax.ShapeDtypeStruct((b, g, n_sub, w), BF16)
    w1_shape = (CMP_BLOCK, HEAD_DIM, CMP_HIDDEN)
    kc, vc = pl.pallas_call(
        _nsa_cmp_body,
        out_shape=(cmp_shape(k_aug), cmp_shape(HEAD_DIM)),
        grid=(b, g),
        in_specs=[seq_in(C_NKV), seq_in(C_NKV + kv),
                  full2((CMP_BLOCK, HEAD_DIM)), full2(w1_shape), full2((CMP_HIDDEN, HEAD_DIM)),
                  full2((CMP_BLOCK, HEAD_DIM)), full2(w1_shape), full2((CMP_HIDDEN, HEAD_DIM)),
                  full2((3, HEAD_DIM))],
        out_specs=(cmp_io(k_aug), cmp_io(HEAD_DIM)),
        scratch_shapes=[pltpu.VMEM((s, HEAD_DIM), F32)],
        compiler_params=_cparams(("parallel", "parallel")),
        name="nsa_cmp",
    )(p3, p3, cmpk_pos, cmpk_w1.reshape(w1_shape).astype(BF16), cmpk_w2.astype(BF16),
      cmpv_pos, cmpv_w1.reshape(w1_shape).astype(BF16), cmpv_w2.astype(BF16), k_norm)

    rows = min(rows, s)
    col = lambda c0: pl.BlockSpec((None, rows, HEAD_DIM),
                                  lambda i, j, t: (i, t, c0 // HEAD_DIM + j))
    key_io = lambda w: pl.BlockSpec((None, None, rows, w), lambda i, j, t: (i, j, t, 0))
    key_shape = lambda w: jax.ShapeDtypeStruct((b, g, s, w), BF16)
    ks, vs, kw, vw = pl.pallas_call(
        _nsa_keys_body,
        out_shape=(key_shape(k_aug), key_shape(HEAD_DIM), key_shape(k_aug), key_shape(HEAD_DIM)),
        grid=(b, g, s // rows),
        in_specs=[col(C_NKV + 2 * kv), col(C_NKV + 3 * kv), col(C_NKV + 4 * kv), col(C_NKV + 5 * kv),
                  pl.BlockSpec((3, HEAD_DIM), lambda i, j, t: (0, 0))],
        out_specs=(key_io(k_aug), key_io(HEAD_DIM), key_io(k_aug), key_io(HEAD_DIM)),
        compiler_params=_cparams(("parallel", "parallel", "parallel")),
        name="nsa_keys",
    )(p3, p3, p3, p3, k_norm)
    return kc, vc, ks, vs, kw, vw


def _split3(x):
    hi = x.astype(BF16).astype(F32)
    mid = (x - hi).astype(BF16).astype(F32)
    return hi, mid, x - hi - mid


def _nsa_body(q_ref, sm_ref, kc_ref, vc_ref, ks_ref, vs_ref, kw_ref, vw_ref,
              qn_ref, ovt_ref, e_ref, tg_ref, o_ref,
              m_ref, l_ref, acc_ref, part_ref, flag_ref, *, top_n):
    grp = pl.program_id(1)
    qb = pl.program_id(2)
    t0 = qb * Q_TILE
    nq = Q_TILE
    nr = NSA_HPG * nq
    ncp = kc_ref.shape[0]
    hrows = [slice(h * nq, (h + 1) * nq) for h in range(NSA_HPG)]
    hcols = [slice(h * HEAD_DIM, (h + 1) * HEAD_DIM) for h in range(NSA_HPG)]

    head = (lax.shift_right_logical(_iota((nr, 1), 0), Q_SHIFT) + grp * NSA_HPG + 1).astype(F32)
    slope = jnp.exp(head * (-ALIBI_MAX / NSA_HEADS * np.log(2.0))) * LOG2E
    lane = _iota((nr, LANES), 1)
    slope_feat = jnp.zeros((nr, LANES), F32)
    for i, term in enumerate(_split3(slope)):
        slope_feat = jnp.where(lane == i, term * float(1 << POS_SHIFT), slope_feat)
        slope_feat = jnp.where(lane == POS_SPLIT + i, term, slope_feat)
    qh = jnp.concatenate(
        [_rms(q_ref[:, hcols[h]].astype(F32), qn_ref[...]) * (HEAD_DIM ** -0.5 * LOG2E)
         for h in range(NSA_HPG)], axis=0)
    q = jnp.concatenate([qh, slope_feat], axis=1).astype(BF16)

    rc = NSA_ROW_CHUNK
    row_chunks = [(h * nq + q0, q0) for q0 in range(0, nq, rc) for h in range(NSA_HPG)]

    def stacked(parts):
        return jnp.concatenate([parts[r0] for r0 in sorted(parts)], axis=0)

    def exp_chunk(s, bias, r0, q0):
        sc = s[r0:r0 + rc] + bias[q0:q0 + rc]
        e = jnp.exp2(sc - jnp.max(sc, axis=-1, keepdims=True))
        return e, jnp.sum(e, axis=-1, keepdims=True)

    gates = jax.nn.sigmoid(sm_ref[...].astype(F32))

    def gate(branch, h):
        at = lambda gv: 2 * DN_HEADS + branch * NSA_HEADS + gv * NSA_HPG + h
        cols = [gates[:, at(gv):at(gv) + 1] for gv in range(NSA_GROUPS)]
        return jnp.where(grp == 0, cols[0], cols[1])

    s_c = _dot_nt(q, kc_ref[...])
    mask_c = (t0 + _iota((nq, ncp), 0)) >= (_iota((nq, ncp), 1) * CMP_STRIDE + (CMP_BLOCK - 1))
    bias_c = jnp.where(mask_c, 0.0, NEG)
    sees_block = (t0 + _iota((nq, 1), 0)) >= CMP_BLOCK - 1
    e_c, inv_c, grp_c = {}, {}, {}
    for r0, q0 in row_chunks:
        e, total = exp_chunk(s_c, bias_c, r0, q0)
        inv_c[r0] = jnp.where(sees_block[q0:q0 + rc], 1.0 / total, 0.0)
        e_c[r0] = e.astype(BF16)
        grp_c[q0] = e * inv_c[r0] + (grp_c[q0] if q0 in grp_c else 0.0)
    o_c = _dot(stacked(e_c), vc_ref[...]) * stacked(inv_c)
    p_grp = stacked(grp_c)

    imp = _dot_nt(ovt_ref[...], p_grp, HIGHEST)
    blk = _iota((LANES, nq), 0)
    cur = lax.shift_right_logical(t0 + _iota((LANES, nq), 1), 6)
    forced = (blk == 0) | (blk == cur) | (blk == cur - 1)
    score = jnp.where(forced, -3e38, jnp.where(blk <= cur, imp, -1e6))
    sel_t = jnp.where(forced, 1.0, 0.0)
    blk_f = blk.astype(F32)

    wk = WINDOW + nq
    w0 = pl.multiple_of(jnp.maximum(t0 - WINDOW, 0), nq)
    s_w = _dot_nt(q, kw_ref[pl.ds(w0, wk), :])
    dist_w = (t0 + _iota((nq, wk), 0)) - (w0 + _iota((nq, wk), 1))
    bias_w = jnp.where((dist_w >= 0) & (dist_w < WINDOW), 0.0, NEG)
    e_w, inv_w = {}, {}
    rounds = top_n - 3
    for r in range(rounds):
        best = jnp.max(score, axis=0, keepdims=True)
        first = jnp.min(jnp.where(score == best, blk_f, float(LANES)), axis=0, keepdims=True)
        pick = blk_f == first
        sel_t = jnp.where(pick, 1.0, sel_t)
        score = jnp.where(pick, -3e38, score)
        for r0, q0 in row_chunks[len(e_w):(r + 1) * len(row_chunks) // rounds]:
            e, total = exp_chunk(s_w, bias_w, r0, q0)
            e_w[r0], inv_w[r0] = e.astype(BF16), 1.0 / total
    eye = jnp.where(_iota((nq, nq), 0) == _iota((nq, nq), 1), 1.0, 0.0).astype(BF16)
    sel = _dot_nt(eye, sel_t.astype(BF16)).astype(BF16)
    o_w = _dot(stacked(e_w), vw_ref[pl.ds(w0, wk), :]) * stacked(inv_w)
    for h in range(NSA_HPG):
        part_ref[:, hcols[h]] = gate(0, h) * o_c[hrows[h]] + gate(2, h) * o_w[hrows[h]]

    picked = jnp.broadcast_to(jnp.sum(sel.astype(F32), axis=0, keepdims=True), (8, LANES))
    tile_count = _dot(picked.astype(BF16), tg_ref[...])
    for t in range(flag_ref.shape[0]):
        flag_ref[t] = (tile_count[0, t] > 0.0).astype(jnp.int32)

    m_ref[...] = jnp.full(m_ref.shape, NEG, F32)
    l_ref[...] = jnp.zeros(l_ref.shape, F32)
    acc_ref[...] = jnp.zeros(acc_ref.shape, F32)
    kt_diag = lax.shift_right_logical(t0, 9)

    def sel_tile(kt, tiles, causal):
        width = tiles * SEL_KT
        k0 = pl.multiple_of(kt * SEL_KT, SEL_KT)
        s = _dot_nt(q, ks_ref[pl.ds(k0, width), :])
        mask = jnp.concatenate([_dot(sel, e_ref[kt + i]) for i in range(tiles)], axis=1) > 0.5
        if causal:
            mask = mask & ((t0 + _iota((nq, width), 0)) >= (k0 + _iota((nq, width), 1)))
        bias = jnp.where(mask, 0.0, NEG)
        es, alphas = {}, {}
        for r0, q0 in row_chunks:
            rows = slice(r0, r0 + rc)
            sc = s[rows] + bias[q0:q0 + rc]
            m_prev = m_ref[rows, :]
            m_new = jnp.maximum(m_prev, jnp.max(sc, axis=-1, keepdims=True))
            alphas[r0] = jnp.exp2(m_prev - m_new)
            e = jnp.exp2(sc - m_new[:, 0:1])
            l_ref[rows, :] = alphas[r0] * l_ref[rows, :] + jnp.sum(e, axis=-1, keepdims=True)
            m_ref[rows, :] = m_new
            es[r0] = e.astype(BF16)
        pv = _dot(stacked(es), vs_ref[pl.ds(k0, width), :])
        acc_ref[...] = stacked(alphas) * acc_ref[...] + pv

    kt_first = jnp.maximum(kt_diag - 1, 0)
    sel_tile(kt_first, 2, True)

    def earlier_tile(i, carry):
        kt = kt_first - 1 - i

        @pl.when(flag_ref[kt] > 0)
        def _():
            sel_tile(kt, 1, False)

        return carry

    lax.fori_loop(0, kt_first, earlier_tile, 0)
    o_s = acc_ref[...] / l_ref[...]
    gates = jax.nn.sigmoid(sm_ref[...].astype(F32))
    for h in range(NSA_HPG):
        o = part_ref[:, hcols[h]] + gate(1, h) * o_s[hrows[h]]
        o_ref[:, hcols[h]] = o.astype(o_ref.dtype)


def _nsa(p3, kc, vc, ks, vs, kw, vw, q_norm):
    b, s, _ = p3.shape
    g = NSA_GROUPS
    ncp = s // CMP_STRIDE
    n_sel = s // SEL_BLOCK
    assert n_sel <= LANES and s % SEL_KT == 0 and s >= WINDOW + Q_TILE and NSA_GROUPS == 2
    assert SEL_KT % Q_TILE == 0 and WINDOW % Q_TILE == 0 and s >= 2 * SEL_KT
    assert min(SEL_TOPN, n_sel) - 3 >= NSA_HPG
    gw = NSA_HPG * HEAD_DIM
    ci = np.arange(ncp)[:, None]
    sj = np.arange(LANES)[None, :]
    per = SEL_BLOCK // CMP_STRIDE
    overlap = ((ci // per == sj).astype(np.float32) + ((ci + 1) // per == sj).astype(np.float32))
    overlap[ncp - 1:] = 0.0
    key_blk = (np.arange(s) // SEL_BLOCK).reshape(s // SEL_KT, 1, SEL_KT)
    expand = (key_blk == np.arange(LANES)[None, :, None]).astype(np.float32)

    n_tiles = s // SEL_KT
    tile_of = (np.arange(LANES)[:, None] // (SEL_KT // SEL_BLOCK)
               == np.arange(LANES)[None, :]).astype(np.float32)

    kv = lambda rows, w: pl.BlockSpec((None, None, rows, w), lambda i, j, t: (i, j, 0, 0))
    full = lambda shp: pl.BlockSpec(shp, lambda i, j, t: (0,) * len(shp))
    return pl.pallas_call(
        functools.partial(_nsa_body, top_n=min(SEL_TOPN, n_sel)),
        out_shape=jax.ShapeDtypeStruct((b, s, NSA_WIDTH), BF16),
        grid=(b, g, s // Q_TILE),
        in_specs=[pl.BlockSpec((None, Q_TILE, gw), lambda i, j, t: (i, t, C_NQ // gw + j)),
                  pl.BlockSpec((None, Q_TILE, LANES), lambda i, j, t: (i, t, C_SMALL // LANES)),
                  kv(ncp, 2 * HEAD_DIM), kv(ncp, HEAD_DIM), kv(s, 2 * HEAD_DIM), kv(s, HEAD_DIM),
                  kv(s, 2 * HEAD_DIM), kv(s, HEAD_DIM),
                  full((1, HEAD_DIM)), full((LANES, ncp)), full((n_tiles, LANES, SEL_KT)),
                  full((LANES, LANES))],
        out_specs=pl.BlockSpec((None, Q_TILE, gw), lambda i, j, t: (i, t, j)),
        scratch_shapes=[pltpu.VMEM((NSA_HPG * Q_TILE, HEAD_DIM), F32)] * 3
        + [pltpu.VMEM((Q_TILE, gw), F32), pltpu.SMEM((n_tiles,), jnp.int32)],
        compiler_params=_cparams(("parallel", "parallel", "arbitrary")),
        name="nsa",
    )(p3, p3, kc, vc, ks, vs, kw, vw, q_norm.reshape(1, -1),
      jnp.asarray(overlap.T), jnp.asarray(expand, dtype=BF16), jnp.asarray(tile_of, dtype=BF16))


def kernel(x, ffn1_norm, ffn1_gate, ffn1_up, ffn1_down, mix_norm, w_in, dn_conv, dn_a_log,
           dn_dt_bias, dn_out_norm, nsa_q_norm, nsa_k_norm, cmpk_pos, cmpk_w1, cmpk_w2,
           cmpv_pos, cmpv_w1, cmpv_w2, sgu_norm_w, sgu_norm_b, sgu_w, sgu_b, w_branch_a,
           w_branch_b, w_branch_c, w_out, ffn2_norm, ffn2_gate, ffn2_up, ffn2_down):
    batch, seq, d = x.shape
    t = batch * seq
    ffn1 = _ffn_weights(ffn1_gate, ffn1_up, ffn1_down)
    ffn2 = _ffn_weights(ffn2_gate, ffn2_up, ffn2_down)
    w_in = _reorder_w_in(w_in)
    mix = tuple(w.astype(BF16) for w in (w_branch_a, w_branch_b, w_branch_c, w_out))

    h = x.reshape(t, d)
    for l in range(ffn1_norm.shape[0]):
        h = _ffn(h, ffn1_norm[l], *ffn1, l)
        p = _inproj(h, mix_norm[l], w_in, l)
        p3 = p.reshape(batch, seq, P_COLS)
        y_a = _deltanet(p3, dn_conv[l], dn_a_log[l], dn_dt_bias[l], dn_out_norm[l])
        kv = _nsa_prep(p3, nsa_k_norm[l], cmpk_pos[l], cmpk_w1[l], cmpk_w2[l],
                       cmpv_pos[l], cmpv_w1[l], cmpv_w2[l])
        y_b = _nsa(p3, *kv, nsa_q_norm[l])
        y_c = _sgu(p, sgu_norm_w[l], sgu_norm_b[l], sgu_w[l], sgu_b[l])
        h = _merge(h, p, y_a.reshape(t, -1), y_b.reshape(t, -1), y_c, *mix, l)
        h = _ffn(h, ffn2_norm[l], *ffn2, l)
    return h.reshape(batch, seq, d)
```

```python
import functools

import jax
import jax.numpy as jnp
import numpy as np
from jax import lax
from jax.experimental import pallas as pl
from jax.experimental.pallas import tpu as pltpu

F32 = jnp.float32
BF16 = jnp.bfloat16
HIGHEST = lax.Precision.HIGHEST

LANES = 128
V7X_VMEM_LIMIT = 56 * 1024 * 1024

HEAD_DIM = 128
EPS = 1e-6
NORM_ROWS = 256
NEG = -1e30

DN_HEADS = 8
DN_WIDTH = DN_HEADS * HEAD_DIM
DN_CONV = 4
DN_CHUNK = 64
DN_TILE = 2 * DN_CHUNK
DN_HALO = 8
DN_STEP_TILES = 2

NSA_HEADS = 8
NSA_GROUPS = 2
NSA_HPG = NSA_HEADS // NSA_GROUPS
NSA_WIDTH = NSA_HEADS * HEAD_DIM
CMP_STRIDE = 16
CMP_BLOCK = 2 * CMP_STRIDE
CMP_HIDDEN = 256
SEL_BLOCK = 64
SEL_TOPN = 16
WINDOW = 512
Q_SHIFT = 8
Q_TILE = 1 << Q_SHIFT
ALIBI_MAX = 8.0
SEL_KT = 512
NSA_ROW_CHUNK = Q_TILE
POS_SHIFT = 6
POS_SPLIT = 3
LOG2E = 1.4426950408889634

SGU_GROUPS = 8
SGU_WIDTH = SGU_GROUPS * HEAD_DIM
SGU_CHUNK = 128


def _cparams(sem):
    return pltpu.CompilerParams(dimension_semantics=sem, vmem_limit_bytes=V7X_VMEM_LIMIT)


def _rms(x, w):
    return x * lax.rsqrt(jnp.mean(x * x, axis=-1, keepdims=True) + EPS) * w


def _iota(shape, dim):
    return lax.broadcasted_iota(jnp.int32, shape, dim)


def _dot(a, b, precision=None):
    return jnp.dot(a, b, preferred_element_type=F32, precision=precision)


def _dot_nt(a, b, precision=None):
    return lax.dot_general(a, b, (((1,), (1,)), ((), ())), preferred_element_type=F32,
                           precision=precision)


def _ffn_body(x_ref, nw_ref, wg_ref, wu_ref, wd_ref, o_ref, h_ref):
    def half_swiglu(h):
        g = _dot(h, wg_ref[...])
        u = _dot(h, wu_ref[...])
        return _dot((0.5 * g * jax.nn.sigmoid(g) * u).astype(BF16), wd_ref[...])

    @pl.when(pl.program_id(1) == 0)
    def _():
        for r0 in range(0, x_ref.shape[0], NORM_ROWS):
            rows = slice(r0, r0 + NORM_ROWS)
            x = x_ref[rows, :]
            h = _rms(x, nw_ref[...]).astype(BF16)
            h_ref[rows, :] = h
            o_ref[rows, :] = x + half_swiglu(h)

    @pl.when(pl.program_id(1) > 0)
    def _():
        o_ref[...] += half_swiglu(h_ref[...])


FF_TILE = 512


def _pad_cols_body(x_ref, o_ref):
    n = x_ref.shape[1]
    o_ref[:, :n] = x_ref[...].astype(BF16)
    o_ref[:, n:] = jnp.zeros((o_ref.shape[0], o_ref.shape[1] - n), BF16)


def _pad_rows_body(x_ref, o_ref, *, n_blocks):
    inside = pl.program_id(1) < n_blocks
    o_ref[...] = jnp.where(inside, x_ref[...], 0.0).astype(BF16)


def _cast_pad(w, axis, extra, *, rows=128):
    nl, r, c = w.shape
    if axis == 2:
        return pl.pallas_call(
            _pad_cols_body,
            out_shape=jax.ShapeDtypeStruct((nl, r, c + extra), BF16),
            grid=(nl, r // rows),
            in_specs=[pl.BlockSpec((None, rows, c), lambda l, i: (l, i, 0))],
            out_specs=pl.BlockSpec((None, rows, c + extra), lambda l, i: (l, i, 0)),
            compiler_params=_cparams(("parallel", "parallel")),
            name="cast_pad_cols",
        )(w)
    assert r % rows == 0 and extra % rows == 0
    n_blocks = r // rows
    return pl.pallas_call(
        functools.partial(_pad_rows_body, n_blocks=n_blocks),
        out_shape=jax.ShapeDtypeStruct((nl, r + extra, c), BF16),
        grid=(nl, (r + extra) // rows),
        in_specs=[pl.BlockSpec((None, rows, c), lambda l, i: (l, jnp.minimum(i, n_blocks - 1), 0))],
        out_specs=pl.BlockSpec((None, rows, c), lambda l, i: (l, i, 0)),
        compiler_params=_cparams(("parallel", "parallel")),
        name="cast_pad_rows",
    )(w)


def _ffn_weights(w_gate, w_up, w_down):
    ff = w_gate.shape[-1]
    extra = FF_TILE * pl.cdiv(ff, FF_TILE) - ff
    return _cast_pad(w_gate, 2, extra), _cast_pad(w_up, 2, extra), _cast_pad(w_down, 1, extra)


def _ffn(x, norm_w, wg, wu, wd, layer, *, tm=1024, tf=FF_TILE):
    t, d = x.shape
    ffp = wg.shape[-1]
    return pl.pallas_call(
        _ffn_body,
        out_shape=jax.ShapeDtypeStruct((t, d), F32),
        grid=(t // tm, ffp // tf),
        in_specs=[
            pl.BlockSpec((tm, d), lambda i, j: (i, 0)),
            pl.BlockSpec((1, d), lambda i, j: (0, 0)),
            pl.BlockSpec((None, d, tf), lambda i, j: (layer, 0, j)),
            pl.BlockSpec((None, d, tf), lambda i, j: (layer, 0, j)),
            pl.BlockSpec((None, tf, d), lambda i, j: (layer, j, 0)),
        ],
        out_specs=pl.BlockSpec((tm, d), lambda i, j: (i, 0)),
        scratch_shapes=[pltpu.VMEM((tm, d), BF16)],
        compiler_params=_cparams(("parallel", "arbitrary")),
        name="ffn",
    )(x, norm_w.reshape(1, d), wg, wu, wd)


C_DQ, C_DK, C_DV, C_DZ = 0, 1024, 2048, 3072
C_NQ = 4096
C_SU, C_SV = 5120, 6144
C_GA, C_GB, C_GC = 7168, 9216, 11264
C_NKV = 13312
C_SMALL = 14848
P_COLS = 15360


W_IN_GROUPS = ((0, 4096),
               (4112, 1024),
               (6696, 2048),
               (8744, 6144),
               (5136, 1536))
W_IN_SMALL = ((4096, 16),
              (6672, 24))


def _reorder_body(x_ref, o_ref):
    rows = o_ref.shape[0]
    dst = 0
    for src, n in W_IN_GROUPS:
        o_ref[:, dst:dst + n] = x_ref[:, src:src + n].astype(BF16)
        dst += n
    small = [x_ref[:, src:src + n] for src, n in W_IN_SMALL]
    small.append(jnp.zeros((rows, LANES - sum(n for _, n in W_IN_SMALL)), F32))
    o_ref[:, dst:dst + LANES] = jnp.concatenate(small, axis=1).astype(BF16)
    o_ref[:, dst + LANES:] = jnp.zeros((rows, o_ref.shape[1] - dst - LANES), BF16)


def _reorder_w_in(w_in, *, rows=128):
    nl, d, n = w_in.shape
    assert sum(w for _, w in W_IN_GROUPS + W_IN_SMALL) == n
    return pl.pallas_call(
        _reorder_body,
        out_shape=jax.ShapeDtypeStruct((nl, d, P_COLS), BF16),
        grid=(nl, d // rows),
        in_specs=[pl.BlockSpec((None, rows, n), lambda l, i: (l, i, 0))],
        out_specs=pl.BlockSpec((None, rows, P_COLS), lambda l, i: (l, i, 0)),
        compiler_params=_cparams(("parallel", "parallel")),
        name="reorder_w_in",
    )(w_in)


def _inproj_body(x_ref, nw_ref, w_ref, o_ref, h_ref):
    @pl.when(pl.program_id(1) == 0)
    def _():
        for r0 in range(0, x_ref.shape[0], NORM_ROWS):
            rows = slice(r0, r0 + NORM_ROWS)
            h = _rms(x_ref[rows, :], nw_ref[...]).astype(BF16)
            h_ref[rows, :] = h
            o_ref[rows, :] = _dot(h, w_ref[...]).astype(o_ref.dtype)

    @pl.when(pl.program_id(1) > 0)
    def _():
        o_ref[...] = _dot(h_ref[...], w_ref[...]).astype(o_ref.dtype)


def _inproj(x, norm_w, w, layer, *, tm=1024, tn=1536):
    t, d = x.shape
    n = w.shape[-1]
    return pl.pallas_call(
        _inproj_body,
        out_shape=jax.ShapeDtypeStruct((t, n), BF16),
        grid=(t // tm, n // tn),
        in_specs=[
            pl.BlockSpec((tm, d), lambda i, j: (i, 0)),
            pl.BlockSpec((1, d), lambda i, j: (0, 0)),
            pl.BlockSpec((None, d, tn), lambda i, j: (layer, 0, j)),
        ],
        out_specs=pl.BlockSpec((tm, tn), lambda i, j: (i, j)),
        scratch_shapes=[pltpu.VMEM((tm, d), BF16)],
        compiler_params=_cparams(("parallel", "arbitrary")),
        name="inproj",
    )(x, norm_w.reshape(1, d), w)


def _merge_body(x_ref, ya_ref, yb_ref, yc_ref, ga_ref, gb_ref, gc_ref,
                wa_ref, wb_ref, wc_ref, wo_ref, o_ref):
    @pl.when(pl.program_id(1) == 0)
    def _():
        o_ref[...] = x_ref[...]

    m = jax.nn.sigmoid(ga_ref[...].astype(F32)) * _dot(ya_ref[...], wa_ref[...])
    m += jax.nn.sigmoid(gb_ref[...].astype(F32)) * _dot(yb_ref[...], wb_ref[...])
    m += jax.nn.sigmoid(gc_ref[...].astype(F32)) * _dot(yc_ref[...], wc_ref[...])
    o_ref[...] += _dot(m.astype(BF16), wo_ref[...])


def _merge(x, p, ya, yb, yc, wa, wb, wc, wo, layer, *, tm=512, tn=1024):
    t, d = x.shape
    k = ya.shape[1]
    y_spec = pl.BlockSpec((tm, k), lambda i, j: (i, 0))
    w_spec = pl.BlockSpec((None, k, tn), lambda i, j: (layer, 0, j))

    def gate_spec(col):
        assert col % tn == 0
        return pl.BlockSpec((tm, tn), lambda i, j: (i, col // tn + j))

    return pl.pallas_call(
        _merge_body,
        out_shape=jax.ShapeDtypeStruct((t, d), F32),
        grid=(t // tm, d // tn),
        in_specs=[pl.BlockSpec((tm, d), lambda i, j: (i, 0)), y_spec, y_spec, y_spec,
                  gate_spec(C_GA), gate_spec(C_GB), gate_spec(C_GC),
                  w_spec, w_spec, w_spec,
                  pl.BlockSpec((None, tn, d), lambda i, j: (layer, j, 0))],
        out_specs=pl.BlockSpec((tm, d), lambda i, j: (i, 0)),
        compiler_params=_cparams(("parallel", "arbitrary")),
        name="merge",
    )(x, ya, yb, yc, p, p, p, wa, wb, wc, wo)


def _sgu_body(u_ref, v_ref, nw_ref, nb_ref, ws_ref, bs_ref, o_ref, *, n_chunk):
    v = jax.nn.gelu(v_ref[...].astype(F32))
    mu = jnp.mean(v, axis=-1, keepdims=True)
    vc = v - mu
    var = jnp.mean(vc * vc, axis=-1, keepdims=True)
    vn = (vc * lax.rsqrt(var + EPS) * nw_ref[...] + nb_ref[...]).astype(BF16)
    tril = _iota((SGU_CHUNK, SGU_CHUNK), 0) >= _iota((SGU_CHUNK, SGU_CHUNK), 1)
    for g in range(SGU_GROUPS):
        cols = slice(g * HEAD_DIM, (g + 1) * HEAD_DIM)
        w = jnp.where(tril, ws_ref[g], 0.0).astype(BF16)
        rhs = jnp.concatenate(
            [vn[c * SGU_CHUNK:(c + 1) * SGU_CHUNK, cols] for c in range(n_chunk)], axis=1)
        s = _dot(w, rhs)
        for c in range(n_chunk):
            rows = slice(c * SGU_CHUNK, (c + 1) * SGU_CHUNK)
            u = jax.nn.gelu(u_ref[rows, cols].astype(F32))
            sc = s[:, c * HEAD_DIM:(c + 1) * HEAD_DIM] + bs_ref[g]
            o_ref[rows, cols] = (u * sc).astype(o_ref.dtype)


def _sgu(p, norm_w, norm_b, w_s, b_s, *, n_chunk=4):
    t = p.shape[0]
    ts = n_chunk * SGU_CHUNK
    bias = jnp.broadcast_to(b_s[:, :, None], (SGU_GROUPS, SGU_CHUNK, HEAD_DIM))
    return pl.pallas_call(
        functools.partial(_sgu_body, n_chunk=n_chunk),
        out_shape=jax.ShapeDtypeStruct((t, SGU_WIDTH), BF16),
        grid=(t // ts,),
        in_specs=[
            pl.BlockSpec((ts, SGU_WIDTH), lambda i: (i, C_SU // SGU_WIDTH)),
            pl.BlockSpec((ts, SGU_WIDTH), lambda i: (i, C_SV // SGU_WIDTH)),
            pl.BlockSpec((1, SGU_WIDTH), lambda i: (0, 0)),
            pl.BlockSpec((1, SGU_WIDTH), lambda i: (0, 0)),
            pl.BlockSpec((SGU_GROUPS, SGU_CHUNK, SGU_CHUNK), lambda i: (0, 0, 0)),
            pl.BlockSpec((SGU_GROUPS, SGU_CHUNK, HEAD_DIM), lambda i: (0, 0, 0)),
        ],
        out_specs=pl.BlockSpec((ts, SGU_WIDTH), lambda i: (i, 0)),
        compiler_params=_cparams(("parallel",)),
        name="sgu",
    )(p, p, norm_w.reshape(1, -1), norm_b.reshape(1, -1), w_s, bias)


def _softplus(x):
    return jnp.maximum(x, 0.0) + jnp.log(1.0 + jnp.exp(-jnp.abs(x)))


def _dn_body(q_ref, k_ref, v_ref, z_ref, sm_ref, conv_ref, alog_ref, dtb_ref, onorm_ref,
             o_ref, xbuf, s_ref):
    n = DN_TILE
    shape = (n, n)
    n_tiles = q_ref.shape[0] // n

    @pl.when(pl.program_id(1) == 0)
    def _():
        xbuf[0:DN_HALO, :] = jnp.zeros((DN_HALO, 3 * DN_WIDTH), F32)
        s_ref[...] = jnp.zeros_like(s_ref)

    xbuf[DN_HALO:, 0:DN_WIDTH] = q_ref[...].astype(F32)
    xbuf[DN_HALO:, DN_WIDTH:2 * DN_WIDTH] = k_ref[...].astype(F32)
    xbuf[DN_HALO:, 2 * DN_WIDTH:] = v_ref[...].astype(F32)

    def conv_silu(r0, c0):
        cols = slice(c0, c0 + HEAD_DIM)
        top = DN_HALO + r0
        acc = conv_ref[DN_CONV - 1:DN_CONV, cols] * xbuf[top:top + n, cols]
        for s in range(1, DN_CONV):
            acc += conv_ref[DN_CONV - 1 - s:DN_CONV - s, cols] * xbuf[top - s:top - s + n, cols]
        return acc * jax.nn.sigmoid(acc)

    row = _iota(shape, 0)
    col = _iota(shape, 1)
    same = lax.shift_right_logical(row, 6) == lax.shift_right_logical(col, 6)
    incl = same & (row >= col)
    strict = same & (row > col)
    first_chunk_cols = col < DN_CHUNK
    eye = jnp.where(row == col, 1.0, 0.0).astype(BF16)
    tri = jnp.where(incl, 1.0, 0.0)
    last = jnp.where(col == jnp.where(row >= DN_CHUNK, n - 1, DN_CHUNK - 1), 1.0, 0.0)

    heads = range(DN_HEADS)
    c = DN_CHUNK

    def prep_units(r0):
        st = dict(lows=[], rhss=[], attns=[], q_decs=[], kd_t0s=[], kd_t1s=[], ends0=[], ends1=[])

        def head_unit(h):
            if h == 0:
                sm = sm_ref[r0:r0 + n, :].astype(F32)
                g_all = -jnp.exp(alog_ref[...]) * _softplus(sm + dtb_ref[...])
                st["beta_all"] = jax.nn.sigmoid(sm)
                st["gc_all"] = _dot(tri, g_all, HIGHEST)
                st["gl_all"] = _dot(last, st["gc_all"], HIGHEST)
                st["gc_t"] = st["gc_all"].T
            qa = conv_silu(r0, h * HEAD_DIM)
            ka = conv_silu(r0, DN_WIDTH + h * HEAD_DIM)
            va = conv_silu(r0, 2 * DN_WIDTH + h * HEAD_DIM)
            qn = qa * lax.rsqrt(jnp.sum(qa * qa, axis=-1, keepdims=True) + EPS) * HEAD_DIM ** -0.5
            kn = ka * lax.rsqrt(jnp.sum(ka * ka, axis=-1, keepdims=True) + EPS)
            beta = jnp.broadcast_to(st["beta_all"][:, DN_HEADS + h:DN_HEADS + h + 1], shape)
            g_col = jnp.broadcast_to(st["gc_all"][:, h:h + 1], shape)
            g_row = jnp.broadcast_to(st["gc_t"][h:h + 1, :], shape)
            g_end = jnp.broadcast_to(st["gl_all"][:, h:h + 1], shape)
            decay = jnp.where(incl, jnp.exp(jnp.where(incl, g_col - g_row, 0.0)), 0.0)
            e_col = jnp.exp(g_col)
            knb = kn.astype(BF16)
            st["lows"].append(jnp.where(strict, beta * _dot_nt(knb, knb) * decay, 0.0))
            st["attns"].append((_dot_nt(qn.astype(BF16), knb) * decay).astype(BF16))
            st["rhss"].append(jnp.concatenate([va * beta, kn * beta * e_col], axis=1))
            st["q_decs"].append((qn * e_col).astype(BF16))
            kd_t = _dot_nt(eye, (kn * jnp.exp(g_end - g_col)).astype(BF16))
            st["kd_t0s"].append(jnp.where(first_chunk_cols, kd_t, 0.0).astype(BF16))
            st["kd_t1s"].append(jnp.where(first_chunk_cols, 0.0, kd_t).astype(BF16))
            decay_end = jnp.exp(g_end)
            st["ends0"].append(decay_end[0:1, :])
            st["ends1"].append(decay_end[c:c + 1, :])

        return st, [functools.partial(head_unit, h) for h in heads]

    def solve_units(st):
        def first():
            st["ms"] = [(-low).astype(BF16) for low in st["lows"]]
            st["xs"] = [rhs + _dot(m, rhs.astype(BF16)) for m, rhs in zip(st["ms"], st["rhss"])]

        def double():
            st["ms"] = [_dot(m, m).astype(BF16) for m in st["ms"]]
            st["xs"] = [x + _dot(m, x.astype(BF16)) for m, x in zip(st["ms"], st["xs"])]

        def split():
            st["us"] = [x[:, :HEAD_DIM] for x in st["xs"]]
            st["ws"] = [x[:, HEAD_DIM:].astype(BF16) for x in st["xs"]]

        return [first] + [double] * 5 + [split]

    def scan_units(st, r0):
        def chunk0():
            st["s0"] = [s_ref[h] for h in heads]
            s0b = [s.astype(BF16) for s in st["s0"]]
            st["v0"] = [st["us"][h][:c] - _dot(st["ws"][h][:c], s0b[h]) for h in heads]
            st["o0"] = [_dot(st["q_decs"][h][:c], s0b[h]) for h in heads]

        def state1():
            vf0 = [jnp.concatenate([v, jnp.zeros_like(v)], axis=0).astype(BF16) for v in st["v0"]]
            st["s1"] = [st["s0"][h] * st["ends0"][h] + _dot(st["kd_t0s"][h], vf0[h]) for h in heads]

        def chunk1():
            s1b = [s.astype(BF16) for s in st["s1"]]
            st["v1"] = [st["us"][h][c:] - _dot(st["ws"][h][c:], s1b[h]) for h in heads]
            st["o1"] = [_dot(st["q_decs"][h][c:], s1b[h]) for h in heads]
            st["vf"] = [jnp.concatenate([st["v0"][h], st["v1"][h]], axis=0).astype(BF16)
                        for h in heads]

        def state2():
            for h in heads:
                s_ref[h] = st["s1"][h] * st["ends1"][h] + _dot(st["kd_t1s"][h], st["vf"][h])

        def output():
            for h in heads:
                hc = slice(h * HEAD_DIM, (h + 1) * HEAD_DIM)
                o = (jnp.concatenate([st["o0"][h], st["o1"][h]], axis=0)
                     + _dot(st["attns"][h], st["vf"][h]))
                z = z_ref[r0:r0 + n, hc].astype(F32)
                o_ref[r0:r0 + n, hc] = (
                    _rms(o, onorm_ref[...]) * (z * jax.nn.sigmoid(z))).astype(o_ref.dtype)

        return [chunk0, state1, chunk1, state2, output]

    def emit(*streams):
        streams = [list(s) for s in streams]
        while any(streams):
            for s in streams:
                if s:
                    s.pop(0)()

    prev, prep = prep_units(0)
    emit(prep)
    pending_scan = []
    for t in range(1, n_tiles):
        st, prep = prep_units(t * n)
        emit(solve_units(prev), prep, pending_scan)
        pending_scan = scan_units(prev, (t - 1) * n)
        prev = st
    emit(solve_units(prev), pending_scan)
    emit(scan_units(prev, (n_tiles - 1) * n))

    xbuf[0:DN_HALO, :] = xbuf[n_tiles * n:n_tiles * n + DN_HALO, :]


def _deltanet(p3, conv_w, a_log, dt_bias, out_norm):
    b, s, _ = p3.shape
    rows = DN_STEP_TILES * DN_TILE
    wide = lambda c: pl.BlockSpec((None, rows, DN_WIDTH), lambda i, t: (i, t, c // DN_WIDTH))
    lane_pad = lambda a: jnp.pad(a.reshape(1, -1), ((0, 0), (0, LANES - a.shape[-1])))
    full = lambda shp: pl.BlockSpec(shp, lambda i, t: (0,) * len(shp))
    return pl.pallas_call(
        _dn_body,
        out_shape=jax.ShapeDtypeStruct((b, s, DN_WIDTH), BF16),
        grid=(b, s // rows),
        in_specs=[wide(C_DQ), wide(C_DK), wide(C_DV), wide(C_DZ),
                  pl.BlockSpec((None, rows, LANES), lambda i, t: (i, t, C_SMALL // LANES)),
                  full((DN_CONV, 3 * DN_WIDTH)), full((1, LANES)), full((1, LANES)),
                  full((1, HEAD_DIM))],
        out_specs=pl.BlockSpec((None, rows, DN_WIDTH), lambda i, t: (i, t, 0)),
        scratch_shapes=[pltpu.VMEM((DN_HALO + rows, 3 * DN_WIDTH), F32),
                        pltpu.VMEM((DN_HEADS, HEAD_DIM, HEAD_DIM), F32)],
        compiler_params=_cparams(("parallel", "arbitrary")),
        name="deltanet",
    )(p3, p3, p3, p3, p3, conv_w, lane_pad(a_log), lane_pad(dt_bias), out_norm.reshape(1, -1))


def _pos_features(n, stride, offset):
    pos = _iota((n, LANES), 0) * stride + offset
    lane = _iota((n, LANES), 1)
    feat = jnp.where(lane < POS_SPLIT, lax.shift_right_logical(pos, POS_SHIFT),
                     jnp.where(lane < 2 * POS_SPLIT, pos & ((1 << POS_SHIFT) - 1), 0))
    return feat.astype(F32).astype(BF16)


def _nsa_keys_body(ks_ref, vs_ref, kw_ref, vw_ref, knorm_ref, ks_o, vs_o, kw_o, vw_o):
    rows = ks_o.shape[0]
    feat = _pos_features(rows, 1, pl.program_id(2) * rows)
    ks_o[:, :HEAD_DIM] = _rms(ks_ref[...].astype(F32), knorm_ref[1:2, :]).astype(BF16)
    ks_o[:, HEAD_DIM:] = feat
    vs_o[...] = vs_ref[...]
    kw_o[:, :HEAD_DIM] = _rms(kw_ref[...].astype(F32), knorm_ref[2:3, :]).astype(BF16)
    kw_o[:, HEAD_DIM:] = feat
    vw_o[:, :HEAD_DIM] = vw_ref[...]
    vw_o[:, HEAD_DIM:] = jnp.ones((rows, HEAD_DIM), BF16)


def _nsa_cmp_body(xk_ref, xv_ref, kpos_ref, kw1_ref, kw2_ref, vpos_ref, vw1_ref, vw2_ref,
                  knorm_ref, kc_o, vc_o, xf_ref):
    n_sub = kc_o.shape[0]

    def compress(x_ref, pos_ref, w1_ref, w2_ref):
        xf_ref[...] = x_ref[...].astype(F32)
        a = jnp.zeros((n_sub, CMP_HIDDEN), F32)
        b = jnp.zeros((n_sub, CMP_HIDDEN), F32)
        for tau in range(CMP_STRIDE):
            x_tau = xf_ref[pl.ds(tau, n_sub, stride=CMP_STRIDE), :]
            lo, hi = tau, CMP_STRIDE + tau
            a += _dot((x_tau + pos_ref[lo:lo + 1, :]).astype(BF16), w1_ref[lo])
            b += _dot((x_tau + pos_ref[hi:hi + 1, :]).astype(BF16), w1_ref[hi])
        hid = a + pltpu.roll(b, n_sub - 1, axis=0)
        hid = hid * jax.nn.sigmoid(hid)
        return _dot(hid.astype(BF16), w2_ref[...])

    kc_o[:, :HEAD_DIM] = _rms(compress(xk_ref, kpos_ref, kw1_ref, kw2_ref),
                              knorm_ref[0:1, :]).astype(BF16)
    kc_o[:, HEAD_DIM:] = _pos_features(n_sub, CMP_STRIDE, CMP_BLOCK - 1)
    vc_o[...] = compress(xv_ref, vpos_ref, vw1_ref, vw2_ref).astype(BF16)


def _nsa_prep(p3, k_norm, cmpk_pos, cmpk_w1, cmpk_w2, cmpv_pos, cmpv_w1, cmpv_w2, *, rows=2048):
    b, s, _ = p3.shape
    g = NSA_GROUPS
    n_sub = s // CMP_STRIDE
    kv = g * HEAD_DIM
    k_aug = 2 * HEAD_DIM

    full2 = lambda shp: pl.BlockSpec(shp, lambda i, j: (0,) * len(shp))
    seq_in = lambda c0: pl.BlockSpec((None, s, HEAD_DIM), lambda i, j: (i, 0, c0 // HEAD_DIM + j))
    cmp_io = lambda w: pl.BlockSpec((None, None, n_sub, w), lambda i, j: (i, j, 0, 0))
    cmp_shape = lambda w: jax.ShapeDtypeStruct((b, g, n_sub, w), BF16)
    w1_shape = (CMP_BLOCK, HEAD_DIM, CMP_HIDDEN)
    kc, vc = pl.pallas_call(
        _nsa_cmp_body,
        out_shape=(cmp_shape(k_aug), cmp_shape(HEAD_DIM)),
        grid=(b, g),
        in_specs=[seq_in(C_NKV), seq_in(C_NKV + kv),
                  full2((CMP_BLOCK, HEAD_DIM)), full2(w1_shape), full2((CMP_HIDDEN, HEAD_DIM)),
                  full2((CMP_BLOCK, HEAD_DIM)), full2(w1_shape), full2((CMP_HIDDEN, HEAD_DIM)),
                  full2((3, HEAD_DIM))],
        out_specs=(cmp_io(k_aug), cmp_io(HEAD_DIM)),
        scratch_shapes=[pltpu.VMEM((s, HEAD_DIM), F32)],
        compiler_params=_cparams(("parallel", "parallel")),
        name="nsa_cmp",
    )(p3, p3, cmpk_pos, cmpk_w1.reshape(w1_shape).astype(BF16), cmpk_w2.astype(BF16),
      cmpv_pos, cmpv_w1.reshape(w1_shape).astype(BF16), cmpv_w2.astype(BF16), k_norm)

    rows = min(rows, s)
    col = lambda c0: pl.BlockSpec((None, rows, HEAD_DIM),
                                  lambda i, j, t: (i, t, c0 // HEAD_DIM + j))
    key_io = lambda w: pl.BlockSpec((None, None, rows, w), lambda i, j, t: (i, j, t, 0))
    key_shape = lambda w: jax.ShapeDtypeStruct((b, g, s, w), BF16)
    ks, vs, kw, vw = pl.pallas_call(
        _nsa_keys_body,
        out_shape=(key_shape(k_aug), key_shape(HEAD_DIM), key_shape(k_aug), key_shape(k_aug)),
        grid=(b, g, s // rows),
        in_specs=[col(C_NKV + 2 * kv), col(C_NKV + 3 * kv), col(C_NKV + 4 * kv), col(C_NKV + 5 * kv),
                  pl.BlockSpec((3, HEAD_DIM), lambda i, j, t: (0, 0))],
        out_specs=(key_io(k_aug), key_io(HEAD_DIM), key_io(k_aug), key_io(k_aug)),
        compiler_params=_cparams(("parallel", "parallel", "parallel")),
        name="nsa_keys",
    )(p3, p3, p3, p3, k_norm)
    return kc, vc, ks, vs, kw, vw


def _split3(x):
    hi = x.astype(BF16).astype(F32)
    mid = (x - hi).astype(BF16).astype(F32)
    return hi, mid, x - hi - mid


def _nsa_body(q_ref, sm_ref, kc_ref, vc_ref, ks_ref, vs_ref, kw_ref, vw_ref,
              qn_ref, ovt_ref, e_ref, tg_ref, o_ref,
              m_ref, l_ref, acc_ref, part_ref, flag_ref, *, top_n):
    grp = pl.program_id(1)
    qb = pl.program_id(2)
    t0 = qb * Q_TILE
    nq = Q_TILE
    nr = NSA_HPG * nq
    ncp = kc_ref.shape[0]
    hrows = [slice(h * nq, (h + 1) * nq) for h in range(NSA_HPG)]
    hcols = [slice(h * HEAD_DIM, (h + 1) * HEAD_DIM) for h in range(NSA_HPG)]

    head = (lax.shift_right_logical(_iota((nr, 1), 0), Q_SHIFT) + grp * NSA_HPG + 1).astype(F32)
    slope = jnp.exp(head * (-ALIBI_MAX / NSA_HEADS * np.log(2.0))) * LOG2E
    lane = _iota((nr, LANES), 1)
    slope_feat = jnp.zeros((nr, LANES), F32)
    for i, term in enumerate(_split3(slope)):
        slope_feat = jnp.where(lane == i, term * float(1 << POS_SHIFT), slope_feat)
        slope_feat = jnp.where(lane == POS_SPLIT + i, term, slope_feat)
    qh = jnp.concatenate(
        [_rms(q_ref[:, hcols[h]].astype(F32), qn_ref[...]) * (HEAD_DIM ** -0.5 * LOG2E)
         for h in range(NSA_HPG)], axis=0)
    q = jnp.concatenate([qh, slope_feat], axis=1).astype(BF16)

    rc = NSA_ROW_CHUNK
    row_chunks = [(h * nq + q0, q0) for q0 in range(0, nq, rc) for h in range(NSA_HPG)]

    def stacked(parts):
        return jnp.concatenate([parts[r0] for r0 in sorted(parts)], axis=0)

    def exp_chunk(s, bias, r0, q0):
        sc = s[r0:r0 + rc] + bias[q0:q0 + rc]
        e = jnp.exp2(sc - jnp.max(sc, axis=-1, keepdims=True))
        return e, jnp.sum(e, axis=-1, keepdims=True)

    gates = jax.nn.sigmoid(sm_ref[...].astype(F32))

    def gate(branch, h):
        at = lambda gv: 2 * DN_HEADS + branch * NSA_HEADS + gv * NSA_HPG + h
        cols = [gates[:, at(gv):at(gv) + 1] for gv in range(NSA_GROUPS)]
        return jnp.where(grp == 0, cols[0], cols[1])

    s_c = _dot_nt(q, kc_ref[...])
    mask_c = (t0 + _iota((nq, ncp), 0)) >= (_iota((nq, ncp), 1) * CMP_STRIDE + (CMP_BLOCK - 1))
    bias_c = jnp.where(mask_c, 0.0, NEG)
    sees_block = (t0 + _iota((nq, 1), 0)) >= CMP_BLOCK - 1
    e_c, inv_c, grp_c = {}, {}, {}
    for r0, q0 in row_chunks:
        e, total = exp_chunk(s_c, bias_c, r0, q0)
        inv_c[r0] = jnp.where(sees_block[q0:q0 + rc], 1.0 / total, 0.0)
        e_c[r0] = e.astype(BF16)
        grp_c[q0] = e * inv_c[r0] + (grp_c[q0] if q0 in grp_c else 0.0)
    o_c = _dot(stacked(e_c), vc_ref[...]) * stacked(inv_c)
    p_grp = stacked(grp_c)

    imp = _dot_nt(ovt_ref[...], p_grp, HIGHEST)
    blk = _iota((LANES, nq), 0)
    cur = lax.shift_right_logical(t0 + _iota((LANES, nq), 1), 6)
    forced = (blk == 0) | (blk == cur) | (blk == cur - 1)
    score = jnp.where(forced, -3e38, jnp.where(blk <= cur, imp, -1e6))
    sel_t = jnp.where(forced, 1.0, 0.0)
    blk_f = blk.astype(F32)

    wk = WINDOW + nq
    w0 = pl.multiple_of(jnp.maximum(t0 - WINDOW, 0), nq)
    s_w = _dot_nt(q, kw_ref[pl.ds(w0, wk), :])
    dist_w = (t0 + _iota((nq, wk), 0)) - (w0 + _iota((nq, wk), 1))
    bias_w = jnp.where((dist_w >= 0) & (dist_w < WINDOW), 0.0, NEG)
    e_w = {}
    rounds = top_n - 3
    for r in range(rounds):
        best = jnp.max(score, axis=0, keepdims=True)
        first = jnp.min(jnp.where(score == best, blk_f, float(LANES)), axis=0, keepdims=True)
        pick = blk_f == first
        sel_t = jnp.where(pick, 1.0, sel_t)
        score = jnp.where(pick, -3e38, score)
        for r0, q0 in row_chunks[len(e_w):(r + 1) * len(row_chunks) // rounds]:
            sc = s_w[r0:r0 + rc] + bias_w[q0:q0 + rc]
            e_w[r0] = jnp.exp2(sc - jnp.max(sc, axis=-1, keepdims=True)).astype(BF16)
    eye = jnp.where(_iota((nq, nq), 0) == _iota((nq, nq), 1), 1.0, 0.0).astype(BF16)
    sel = _dot_nt(eye, sel_t.astype(BF16)).astype(BF16)
    pv_w = _dot(stacked(e_w), vw_ref[pl.ds(w0, wk), :])
    o_w = pv_w[:, :HEAD_DIM] / pv_w[:, HEAD_DIM:]
    for h in range(NSA_HPG):
        part_ref[:, hcols[h]] = gate(0, h) * o_c[hrows[h]] + gate(2, h) * o_w[hrows[h]]

    picked = jnp.broadcast_to(jnp.sum(sel.astype(F32), axis=0, keepdims=True), (8, LANES))
    tile_count = _dot(picked.astype(BF16), tg_ref[...])
    for t in range(flag_ref.shape[0]):
        flag_ref[t] = (tile_count[0, t] > 0.0).astype(jnp.int32)

    m_ref[...] = jnp.full(m_ref.shape, NEG, F32)
    l_ref[...] = jnp.zeros(l_ref.shape, F32)
    acc_ref[...] = jnp.zeros(acc_ref.shape, F32)
    kt_diag = lax.shift_right_logical(t0, 9)

    def sel_tile(kt, tiles, causal):
        width = tiles * SEL_KT
        k0 = pl.multiple_of(kt * SEL_KT, SEL_KT)
        s = _dot_nt(q, ks_ref[pl.ds(k0, width), :])
        mask = jnp.concatenate([_dot(sel, e_ref[kt + i]) for i in range(tiles)], axis=1) > 0.5
        if causal:
            mask = mask & ((t0 + _iota((nq, width), 0)) >= (k0 + _iota((nq, width), 1)))
        bias = jnp.where(mask, 0.0, NEG)
        es, alphas = {}, {}
        for r0, q0 in row_chunks:
            rows = slice(r0, r0 + rc)
            sc = s[rows] + bias[q0:q0 + rc]
            m_prev = m_ref[rows, :]
            m_new = jnp.maximum(m_prev, jnp.max(sc, axis=-1, keepdims=True))
            alphas[r0] = jnp.exp2(m_prev - m_new)
            e = jnp.exp2(sc - m_new[:, 0:1])
            l_ref[rows, :] = alphas[r0] * l_ref[rows, :] + jnp.sum(e, axis=-1, keepdims=True)
            m_ref[rows, :] = m_new
            es[r0] = e.astype(BF16)
        pv = _dot(stacked(es), vs_ref[pl.ds(k0, width), :])
        acc_ref[...] = stacked(alphas) * acc_ref[...] + pv

    kt_first = jnp.maximum(kt_diag - 1, 0)
    sel_tile(kt_first, 2, True)

    def earlier_tile(i, carry):
        kt = kt_first - 1 - i

        @pl.when(flag_ref[kt] > 0)
        def _():
            sel_tile(kt, 1, False)

        return carry

    lax.fori_loop(0, kt_first, earlier_tile, 0)
    o_s = acc_ref[...] / l_ref[...]
    gates = jax.nn.sigmoid(sm_ref[...].astype(F32))
    for h in range(NSA_HPG):
        o = part_ref[:, hcols[h]] + gate(1, h) * o_s[hrows[h]]
        o_ref[:, hcols[h]] = o.astype(o_ref.dtype)


def _nsa(p3, kc, vc, ks, vs, kw, vw, q_norm):
    b, s, _ = p3.shape
    g = NSA_GROUPS
    ncp = s // CMP_STRIDE
    n_sel = s // SEL_BLOCK
    assert n_sel <= LANES and s % SEL_KT == 0 and s >= WINDOW + Q_TILE and NSA_GROUPS == 2
    assert SEL_KT % Q_TILE == 0 and WINDOW % Q_TILE == 0 and s >= 2 * SEL_KT
    assert min(SEL_TOPN, n_sel) - 3 >= NSA_HPG
    gw = NSA_HPG * HEAD_DIM
    ci = np.arange(ncp)[:, None]
    sj = np.arange(LANES)[None, :]
    per = SEL_BLOCK // CMP_STRIDE
    overlap = ((ci // per == sj).astype(np.float32) + ((ci + 1) // per == sj).astype(np.float32))
    overlap[ncp - 1:] = 0.0
    key_blk = (np.arange(s) // SEL_BLOCK).reshape(s // SEL_KT, 1, SEL_KT)
    expand = (key_blk == np.arange(LANES)[None, :, None]).astype(np.float32)

    n_tiles = s // SEL_KT
    tile_of = (np.arange(LANES)[:, None] // (SEL_KT // SEL_BLOCK)
               == np.arange(LANES)[None, :]).astype(np.float32)

    kv = lambda rows, w: pl.BlockSpec((None, None, rows, w), lambda i, j, t: (i, j, 0, 0))
    full = lambda shp: pl.BlockSpec(shp, lambda i, j, t: (0,) * len(shp))
    return pl.pallas_call(
        functools.partial(_nsa_body, top_n=min(SEL_TOPN, n_sel)),
        out_shape=jax.ShapeDtypeStruct((b, s, NSA_WIDTH), BF16),
        grid=(b, g, s // Q_TILE),
        in_specs=[pl.BlockSpec((None, Q_TILE, gw), lambda i, j, t: (i, t, C_NQ // gw + j)),
                  pl.BlockSpec((None, Q_TILE, LANES), lambda i, j, t: (i, t, C_SMALL // LANES)),
                  kv(ncp, 2 * HEAD_DIM), kv(ncp, HEAD_DIM), kv(s, 2 * HEAD_DIM), kv(s, HEAD_DIM),
                  kv(s, 2 * HEAD_DIM), kv(s, 2 * HEAD_DIM),
                  full((1, HEAD_DIM)), full((LANES, ncp)), full((n_tiles, LANES, SEL_KT)),
                  full((LANES, LANES))],
        out_specs=pl.BlockSpec((None, Q_TILE, gw), lambda i, j, t: (i, t, j)),
        scratch_shapes=[pltpu.VMEM((NSA_HPG * Q_TILE, HEAD_DIM), F32)] * 3
        + [pltpu.VMEM((Q_TILE, gw), F32), pltpu.SMEM((n_tiles,), jnp.int32)],
        compiler_params=_cparams(("parallel", "parallel", "arbitrary")),
        name="nsa",
    )(p3, p3, kc, vc, ks, vs, kw, vw, q_norm.reshape(1, -1),
      jnp.asarray(overlap.T), jnp.asarray(expand, dtype=BF16), jnp.asarray(tile_of, dtype=BF16))


def kernel(x, ffn1_norm, ffn1_gate, ffn1_up, ffn1_down, mix_norm, w_in, dn_conv, dn_a_log,
           dn_dt_bias, dn_out_norm, nsa_q_norm, nsa_k_norm, cmpk_pos, cmpk_w1, cmpk_w2,
           cmpv_pos, cmpv_w1, cmpv_w2, sgu_norm_w, sgu_norm_b, sgu_w, sgu_b, w_branch_a,
           w_branch_b, w_branch_c, w_out, ffn2_norm, ffn2_gate, ffn2_up, ffn2_down):
    batch, seq, d = x.shape
    t = batch * seq
    ffn1 = _ffn_weights(ffn1_gate, ffn1_up, ffn1_down)
    ffn2 = _ffn_weights(ffn2_gate, ffn2_up, ffn2_down)
    w_in = _reorder_w_in(w_in)
    mix = tuple(w.astype(BF16) for w in (w_branch_a, w_branch_b, w_branch_c, w_out))

    h = x.reshape(t, d)
    for l in range(ffn1_norm.shape[0]):
        h = _ffn(h, ffn1_norm[l], *ffn1, l)
        p = _inproj(h, mix_norm[l], w_in, l)
        p3 = p.reshape(batch, seq, P_COLS)
        y_a = _deltanet(p3, dn_conv[l], dn_a_log[l], dn_dt_bias[l], dn_out_norm[l])
        kv = _nsa_prep(p3, nsa_k_norm[l], cmpk_pos[l], cmpk_w1[l], cmpk_w2[l],
                       cmpv_pos[l], cmpv_w1[l], cmpv_w2[l])
        y_b = _nsa(p3, *kv, nsa_q_norm[l])
        y_c = _sgu(p, sgu_norm_w[l], sgu_norm_b[l], sgu_w[l], sgu_b[l])
        h = _merge(h, p, y_a.reshape(t, -1), y_b.reshape(t, -1), y_c, *mix, l)
        h = _ffn(h, ffn2_norm[l], *ffn2, l)
    return h.reshape(batch, seq, d)
```
